```python
import jax, jax.numpy as jnp
from jax import lax
import numpy as np

D_MODEL = 1024
BATCH = 4
SEQ = 8192
DEPTH = 1

CHUNK = 64
Q_BLOCK = 128
DIFF_HEADS = 4
DIFF_HEAD_DIM = 64
DIFF_V_DIM = 2 * DIFF_HEAD_DIM
DIFF_WIDTH = DIFF_HEADS * DIFF_V_DIM
RET_HEADS = 4
RET_KEY_DIM = 64
RET_V_DIM = 128
RET_WIDTH = RET_HEADS * RET_V_DIM
MIX_WIDTH = DIFF_WIDTH + RET_WIDTH
D_FF = 2816
CONV_WIDTH = 3
EPS = 1e-6
IN_SPLIT = (DIFF_HEADS * 2 * DIFF_HEAD_DIM, DIFF_HEADS * 2 * DIFF_HEAD_DIM, DIFF_WIDTH,
            RET_HEADS * RET_KEY_DIM, RET_HEADS * RET_KEY_DIM, RET_WIDTH, RET_WIDTH)
IN_WIDTH = sum(IN_SPLIT)

kernel_name = "hymba_style_diffattn_retention_convffn"


def rmsnorm(x, g):
    xf = x.astype(jnp.float32)
    y = xf * lax.rsqrt(jnp.mean(xf * xf, axis=-1, keepdims=True) + EPS)
    return (y * g.astype(jnp.float32)).astype(x.dtype)


def alibi_slopes(n_heads):
    return jnp.exp2(-8.0 * jnp.arange(1, n_heads + 1, dtype=jnp.float32) / n_heads)


def retention_log_gamma(n_heads):
    return jnp.log1p(-jnp.exp2(-5.0 - jnp.arange(n_heads, dtype=jnp.float32)))


def diff_attention(q, k, v, lam, lam_init, subln_g):
    B, S = q.shape[0], q.shape[1]
    nb = S // Q_BLOCK
    q = q * (DIFF_HEAD_DIM ** -0.5)
    qb = q.reshape(B, nb, Q_BLOCK, DIFF_HEADS, 2, DIFF_HEAD_DIM).transpose(1, 0, 2, 3, 4, 5)
    slopes = alibi_slopes(DIFF_HEADS)
    kpos = jnp.arange(S)

    def block(args):
        q_blk, b = args
        qpos = b * Q_BLOCK + jnp.arange(Q_BLOCK)
        s = jnp.einsum('bqhid,bkhid->bihqk', q_blk, k).astype(jnp.float32)
        dist = jnp.abs(qpos[:, None] - kpos[None, :]).astype(jnp.float32)
        allowed = (kpos[None, :] // CHUNK) <= (qpos[:, None] // CHUNK)
        bias = jnp.where(allowed[None], -slopes[:, None, None] * dist[None], -jnp.inf)
        p = jax.nn.softmax(s + bias, axis=-1)
        a = p[:, 0] - lam * p[:, 1]
        return jnp.einsum('bhqk,bkhe->bqhe', a.astype(v.dtype), v)

    o = lax.map(block, (qb, jnp.arange(nb)))
    o = o.transpose(1, 0, 2, 3, 4).reshape(B, S, DIFF_HEADS, DIFF_V_DIM)
    o = rmsnorm(o, subln_g) * (1.0 - lam_init)
    return o.reshape(B, S, DIFF_WIDTH)


def retention(q, k, v):
    B, S = q.shape[0], q.shape[1]
    nc = S // CHUNK
    log_g = retention_log_gamma(RET_HEADS)
    q = q.reshape(B, nc, CHUNK, RET_HEADS, RET_KEY_DIM)
    k = k.reshape(B, nc, CHUNK, RET_HEADS, RET_KEY_DIM) * (RET_KEY_DIM ** -0.5)
    v = v.reshape(B, nc, CHUNK, RET_HEADS, RET_V_DIM)
    n = jnp.arange(CHUNK, dtype=jnp.float32)
    intra_decay = jnp.exp(log_g[:, None, None] * jnp.abs(n[:, None] - n[None, :]))
    s = jnp.einsum('bcnhd,bcmhd->bchnm', q, k) * intra_decay
    o_intra = jnp.einsum('bchnm,bcmhe->bcnhe', s, v)
    k_decay = jnp.exp(log_g[:, None] * (CHUNK - 1.0 - n))
    kv = jnp.einsum('bcmhd,bcmhe,hm->cbhde', k, v, k_decay)
    chunk_decay = jnp.exp(log_g * CHUNK)[None, :, None, None]

    def step(state, kv_c):
        return chunk_decay * state + kv_c, state

    _, s_prev = lax.scan(step, jnp.zeros_like(kv[0]), kv)
    q_decay = jnp.exp(log_g[:, None] * (n + 1.0))
    o_cross = jnp.einsum('bcnhd,cbhde,hn->bcnhe', q, s_prev, q_decay)
    return (o_intra + o_cross).reshape(B, S, RET_HEADS, RET_V_DIM)


def head_groupnorm(o, g):
    of = o.astype(jnp.float32)
    mu = jnp.mean(of, axis=-1, keepdims=True)
    var = jnp.mean(jnp.square(of - mu), axis=-1, keepdims=True)
    y = (of - mu) * lax.rsqrt(var + EPS)
    return y.reshape(o.shape[0], o.shape[1], -1) * g.astype(jnp.float32)


def conv_ffn(x, w_up, conv_w, conv_b, w_down):
    h = x @ w_up
    a, b = jnp.split(h, 2, axis=-1)
    a = lax.conv_general_dilated(a, conv_w[:, None, :].astype(a.dtype), window_strides=(1,),
                                 padding=[(CONV_WIDTH - 1, 0)],
                                 dimension_numbers=('NWC', 'WIO', 'NWC'),
                                 feature_group_count=D_FF) + conv_b
    return (jax.nn.gelu(a) * b) @ w_down


def setup_inputs(seed: int = 0) -> dict:
    key = jax.random.key(seed)
    ks = jax.random.split(key, 20)
    nrm = lambda k, shape, s: jax.random.normal(k, shape, jnp.float32) * s
    L = DEPTH
    return {
        "x": nrm(ks[0], (BATCH, SEQ, D_MODEL), 1.0),
        "norm_mix_g": 1.0 + nrm(ks[1], (L, D_MODEL), 0.02),
        "w_in": nrm(ks[2], (L, D_MODEL, IN_WIDTH), D_MODEL ** -0.5),
        "lambda_q1": nrm(ks[3], (L, DIFF_HEAD_DIM), 0.1),
        "lambda_k1": nrm(ks[4], (L, DIFF_HEAD_DIM), 0.1),
        "lambda_q2": nrm(ks[5], (L, DIFF_HEAD_DIM), 0.1),
        "lambda_k2": nrm(ks[6], (L, DIFF_HEAD_DIM), 0.1),
        "diff_subln_g": 1.0 + nrm(ks[7], (L, DIFF_V_DIM), 0.02),
        "ret_gn_g": 1.0 + nrm(ks[8], (L, RET_WIDTH), 0.02),
        "w_out": nrm(ks[9], (L, MIX_WIDTH, D_MODEL), MIX_WIDTH ** -0.5),
        "norm_ffn_g": 1.0 + nrm(ks[10], (L, D_MODEL), 0.02),
        "w_up": nrm(ks[11], (L, D_MODEL, 2 * D_FF), D_MODEL ** -0.5),
        "conv_w": nrm(ks[12], (L, CONV_WIDTH, D_FF), CONV_WIDTH ** -0.5),
        "conv_b": nrm(ks[13], (L, D_FF), 0.02),
        "w_down": nrm(ks[14], (L, D_FF, D_MODEL), D_FF ** -0.5),
        "final_norm_g": 1.0 + nrm(ks[15], (D_MODEL,), 0.02),
    }


def reference(x, norm_mix_g, w_in, lambda_q1, lambda_k1, lambda_q2, lambda_k2, diff_subln_g,
              ret_gn_g, w_out, norm_ffn_g, w_up, conv_w, conv_b, w_down, final_norm_g):
    B, S, _ = x.shape
    split_idx = [int(i) for i in np.cumsum(IN_SPLIT)[:-1]]
    for l in range(DEPTH):
        lam_init = 0.8 - 0.6 * float(np.exp(-0.3 * l))
        h = rmsnorm(x, norm_mix_g[l])
        proj = h @ w_in[l]
        dq, dk, dv, rq, rk, rv, rg = jnp.split(proj, split_idx, axis=-1)
        lam = (jnp.exp(jnp.sum(lambda_q1[l].astype(jnp.float32) * lambda_k1[l].astype(jnp.float32)))
               - jnp.exp(jnp.sum(lambda_q2[l].astype(jnp.float32) * lambda_k2[l].astype(jnp.float32)))
               + lam_init)
        o_diff = diff_attention(dq.reshape(B, S, DIFF_HEADS, 2, DIFF_HEAD_DIM),
                                dk.reshape(B, S, DIFF_HEADS, 2, DIFF_HEAD_DIM),
                                dv.reshape(B, S, DIFF_HEADS, DIFF_V_DIM),
                                lam, lam_init, diff_subln_g[l])
        o_ret = retention(rq.reshape(B, S, RET_HEADS, RET_KEY_DIM).astype(jnp.float32),
                          rk.reshape(B, S, RET_HEADS, RET_KEY_DIM).astype(jnp.float32),
                          rv.reshape(B, S, RET_HEADS, RET_V_DIM).astype(jnp.float32))
        o_ret = (head_groupnorm(o_ret, ret_gn_g[l]) * jax.nn.silu(rg.astype(jnp.float32))).astype(x.dtype)
        x = x + jnp.concatenate([o_diff.astype(x.dtype), o_ret], axis=-1) @ w_out[l]
        x = x + conv_ffn(rmsnorm(x, norm_ffn_g[l]), w_up[l], conv_w[l], conv_b[l], w_down[l])
    return rmsnorm(x, final_norm_g)
```

```python
import functools
import math

import jax
import jax.numpy as jnp
import numpy as np
from jax import lax
from jax.experimental import pallas as pl
from jax.experimental.pallas import tpu as pltpu

D_MODEL = 1024
CHUNK = 64
DIFF_HEADS = 4
DIFF_HEAD_DIM = 64
DIFF_V_DIM = 128
DIFF_WIDTH = 512
RET_HEADS = 4
RET_KEY_DIM = 64
RET_V_DIM = 128
RET_WIDTH = 512
D_FF = 2816
IN_WIDTH = 3072
EPS = 1e-6
LAM_INIT = 0.8 - 0.6 * math.exp(-0.3 * 0)
LOG2E = math.log2(math.e)
MASKED = -1e30

COL_DQ, COL_DK, COL_DV = 0, 4, 8
COLB_RQ, COLB_RK = 6, 7
COLB_RV, COLB_RG = 4, 5

TM_IN = 512
T_ATT = 256
T_RET = 256
TM_FFN = 512
TF = 256
NF = D_FF // TF
VMEM_LIMIT = 56 * 1024 * 1024

BF16 = jnp.bfloat16
F32 = jnp.float32


def _rms(x, g):
    ms = jnp.mean(x * x, axis=-1, keepdims=True)
    return x * lax.rsqrt(ms + EPS) * g


def _inproj_kernel(x_ref, g_ref, w_ref, o_ref):
    h = _rms(x_ref[...], g_ref[...]).astype(BF16)
    o_ref[...] = jnp.dot(h, w_ref[...], preferred_element_type=F32).astype(o_ref.dtype)


def _inproj(x2, g, w_bf16):
    n = x2.shape[0]
    return pl.pallas_call(
        _inproj_kernel,
        grid=(n // TM_IN,),
        in_specs=[
            pl.BlockSpec((TM_IN, D_MODEL), lambda i: (i, 0)),
            pl.BlockSpec((1, D_MODEL), lambda i: (0, 0)),
            pl.BlockSpec((D_MODEL, IN_WIDTH), lambda i: (0, 0), pipeline_mode=pl.Buffered(1)),
        ],
        out_specs=pl.BlockSpec((TM_IN, IN_WIDTH), lambda i: (i, 0)),
        out_shape=jax.ShapeDtypeStruct((n, IN_WIDTH), BF16),
        compiler_params=pltpu.CompilerParams(
            dimension_semantics=("arbitrary",), vmem_limit_bytes=VMEM_LIMIT),
        name="inproj",
    )(x2, g, w_bf16)


def _attn_kernel(q_ref, k_ref, v_ref, dbias_ref, cbias_ref, tslope_ref,
                 lq1_ref, lk1_ref, lq2_ref, lk2_ref, g_ref, o_ref,
                 q2_scr, m_scr, l_scr, acc_scr):
    T = T_ATT
    i = pl.program_id(2)

    q = q_ref[...].astype(F32) * (DIFF_HEAD_DIM ** -0.5 * LOG2E)
    lane = lax.broadcasted_iota(jnp.int32, (T, 2 * DIFF_HEAD_DIM), 1)
    q2_scr[0:T, :] = jnp.where(lane < DIFF_HEAD_DIM, q, 0.0).astype(BF16)
    q2_scr[T:2 * T, :] = jnp.where(lane >= DIFF_HEAD_DIM, q, 0.0).astype(BF16)
    m_scr[...] = jnp.full(m_scr.shape, MASKED, F32)
    l_scr[...] = jnp.zeros(l_scr.shape, F32)
    acc_scr[...] = jnp.zeros(acc_scr.shape, F32)

    def step(j, bias):
        start = pl.multiple_of(j * T, T)
        k = k_ref[pl.ds(start, T), :]
        v = v_ref[pl.ds(start, T), :]
        s = lax.dot_general(q2_scr[...], k, (((1,), (1,)), ((), ())),
                            preferred_element_type=F32)
        s = (s.reshape(2, T, T) + bias).reshape(2 * T, T)
        m_prev = m_scr[...]
        m_new = jnp.maximum(m_prev, jnp.max(s, axis=1, keepdims=True))
        alpha = jnp.exp2(m_prev - m_new)
        p = jnp.exp2(s - m_new)
        l_scr[...] = alpha * l_scr[...] + jnp.sum(p, axis=1, keepdims=True)
        acc_scr[...] = alpha * acc_scr[...] + jnp.dot(
            p.astype(BF16), v, preferred_element_type=F32)
        m_scr[...] = m_new

    def past_tile(j, carry):
        bias = cbias_ref[0] + tslope_ref[0] * (j - i).astype(F32)
        step(j, bias[None])
        return carry

    lax.fori_loop(0, i, past_tile, 0)
    step(i, dbias_ref[...])

    lam = (jnp.exp(jnp.sum(lq1_ref[...] * lk1_ref[...], axis=1, keepdims=True))
           - jnp.exp(jnp.sum(lq2_ref[...] * lk2_ref[...], axis=1, keepdims=True))
           + LAM_INIT)
    inv_l = 1.0 / l_scr[...]
    o = acc_scr[0:T, :] * inv_l[0:T] - lam * (acc_scr[T:2 * T, :] * inv_l[T:2 * T])
    o_ref[...] = (_rms(o, g_ref[...]) * (1.0 - LAM_INIT)).astype(o_ref.dtype)


def _attn_tables():
    T = T_ATT
    slopes = np.exp2(-8.0 * np.arange(1, DIFF_HEADS + 1, dtype=np.float64) / DIFF_HEADS) * LOG2E
    r = np.arange(T)[:, None]
    c = np.arange(T)[None, :]
    allowed = (c // CHUNK) <= (r // CHUNK)
    rel = (r - np.abs(r - c)).astype(np.float64)
    dbias = np.where(allowed[None], slopes[:, None, None] * rel[None], MASKED)
    cbias = slopes[:, None, None] * np.arange(T, dtype=np.float64)[None, None, :]
    tslope = np.broadcast_to((slopes * T)[:, None, None], (DIFF_HEADS, 1, T))
    return (jnp.asarray(dbias, F32), jnp.asarray(cbias, F32), jnp.asarray(tslope, F32))


def _diff_attention(proj, batch, seq, lq1, lk1, lq2, lk2, subln_g):
    T = T_ATT
    nq = seq // T
    dbias, cbias, tslope = _attn_tables()
    small = lambda shape: pl.BlockSpec(shape, lambda b, h, i: (0,) * len(shape))
    return pl.pallas_call(
        _attn_kernel,
        grid=(batch, DIFF_HEADS, nq),
        in_specs=[
            pl.BlockSpec((T, 128), lambda b, h, i: (b * nq + i, COL_DQ + h)),
            pl.BlockSpec((seq, 128), lambda b, h, i: (b, COL_DK + h)),
            pl.BlockSpec((seq, 128), lambda b, h, i: (b, COL_DV + h)),
            pl.BlockSpec((1, T, T), lambda b, h, i: (h, 0, 0)),
            pl.BlockSpec((1, 1, T), lambda b, h, i: (h, 0, 0)),
            pl.BlockSpec((1, 1, T), lambda b, h, i: (h, 0, 0)),
            small((1, DIFF_HEAD_DIM)), small((1, DIFF_HEAD_DIM)),
            small((1, DIFF_HEAD_DIM)), small((1, DIFF_HEAD_DIM)),
            small((1, DIFF_V_DIM)),
        ],
        out_specs=pl.BlockSpec((T, DIFF_V_DIM), lambda b, h, i: (b * nq + i, h)),
        out_shape=jax.ShapeDtypeStruct((batch * seq, DIFF_WIDTH), BF16),
        scratch_shapes=[
            pltpu.VMEM((2 * T, 128), BF16),
            pltpu.VMEM((2 * T, 1), F32),
            pltpu.VMEM((2 * T, 1), F32),
            pltpu.VMEM((2 * T, DIFF_V_DIM), F32),
        ],
        compiler_params=pltpu.CompilerParams(
            dimension_semantics=("arbitrary", "arbitrary", "arbitrary"),
            vmem_limit_bytes=VMEM_LIMIT),
        name="diff_attn",
    )(proj, proj, proj, dbias, cbias, tslope, lq1, lk1, lq2, lk2, subln_g)


def _ret_kernel(q_ref, k_ref, v_ref, gate_ref, dmat_ref, qdec_ref, kdec_ref, cdec_ref,
                gn_ref, o_ref, state_scr):
    T = T_RET

    @pl.when(pl.program_id(1) == 0)
    def _():
        state_scr[...] = jnp.zeros(state_scr.shape, F32)

    lane = lax.broadcasted_iota(jnp.int32, (T, 128), 1)
    for h in range(RET_HEADS):
        pair, half = divmod(h, 2)
        qp = q_ref[:, 128 * pair:128 * (pair + 1)]
        kp = k_ref[:, 128 * pair:128 * (pair + 1)]
        in_head = (lane >= RET_KEY_DIM) if half else (lane < RET_KEY_DIM)
        km = jnp.where(in_head, kp, jnp.zeros_like(kp))
        v = v_ref[:, 128 * h:128 * (h + 1)]
        s = lax.dot_general(qp, km, (((1,), (1,)), ((), ())), preferred_element_type=F32)
        s = s * dmat_ref[h]
        o = jnp.dot(s.astype(BF16), v, preferred_element_type=F32)
        state = state_scr[h]
        o = o + qdec_ref[h] * jnp.dot(qp, state.astype(BF16), preferred_element_type=F32)
        vd = (v.astype(F32) * kdec_ref[h]).astype(BF16)
        state_scr[h] = cdec_ref[h] * state + lax.dot_general(
            km, vd, (((0,), (0,)), ((), ())), preferred_element_type=F32)

        mu = jnp.mean(o, axis=-1, keepdims=True)
        d = o - mu
        var = jnp.mean(d * d, axis=-1, keepdims=True)
        y = d * lax.rsqrt(var + EPS) * gn_ref[:, 128 * h:128 * (h + 1)]
        gate = gate_ref[:, 128 * h:128 * (h + 1)].astype(F32)
        y = y * (gate * (1.0 / (1.0 + jnp.exp(-gate))))
        o_ref[:, 128 * h:128 * (h + 1)] = y.astype(o_ref.dtype)


def _ret_tables():
    T = T_RET
    log_g = np.log1p(-np.exp2(-5.0 - np.arange(RET_HEADS, dtype=np.float64)))
    n = np.arange(T, dtype=np.float64)
    kscale = RET_KEY_DIM ** -0.5
    allowed = (np.arange(T)[None, :] // CHUNK) <= (np.arange(T)[:, None] // CHUNK)
    dmat = np.where(allowed[None],
                    np.exp(log_g[:, None, None] * np.abs(n[:, None] - n[None, :])[None]), 0.0) * kscale
    qdec = np.exp(log_g[:, None] * (n + 1.0)[None]) * kscale
    kdec = np.exp(log_g[:, None] * (T - 1.0 - n)[None])
    cdec = np.exp(log_g * T)
    qdec = np.broadcast_to(qdec[:, :, None], (RET_HEADS, T, 128))
    kdec = np.broadcast_to(kdec[:, :, None], (RET_HEADS, T, 128))
    cdec = np.broadcast_to(cdec[:, None, None], (RET_HEADS, 1, 128))
    return tuple(jnp.asarray(a, F32) for a in (dmat, qdec, kdec, cdec))


def _retention(proj, batch, seq, gn_g):
    T = T_RET
    nt = seq // T
    dmat, qdec, kdec, cdec = _ret_tables()
    const = lambda shape: pl.BlockSpec(shape, lambda b, t: (0,) * len(shape))
    return pl.pallas_call(
        _ret_kernel,
        grid=(batch, nt),
        in_specs=[
            pl.BlockSpec((T, 256), lambda b, t: (b * nt + t, COLB_RQ)),
            pl.BlockSpec((T, 256), lambda b, t: (b * nt + t, COLB_RK)),
            pl.BlockSpec((T, 512), lambda b, t: (b * nt + t, COLB_RV)),
            pl.BlockSpec((T, 512), lambda b, t: (b * nt + t, COLB_RG)),
            const((RET_HEADS, T, T)), const((RET_HEADS, T, 128)),
            const((RET_HEADS, T, 128)), const((RET_HEADS, 1, 128)),
            const((1, RET_WIDTH)),
        ],
        out_specs=pl.BlockSpec((T, RET_WIDTH), lambda b, t: (b * nt + t, 0)),
        out_shape=jax.ShapeDtypeStruct((batch * seq, RET_WIDTH), BF16),
        scratch_shapes=[pltpu.VMEM((RET_HEADS, 128, 128), F32)],
        compiler_params=pltpu.CompilerParams(
            dimension_semantics=("arbitrary", "arbitrary"), vmem_limit_bytes=VMEM_LIMIT),
        name="retention",
    )(proj, proj, proj, proj, dmat, qdec, kdec, cdec, gn_g)


def _ffn_kernel(tiles_per_seq, x_ref, od_ref, or_ref, wo_ref, g2_ref, wa_ref, wb_ref,
                cw_ref, cb_ref, wd_ref, gf_ref, o_ref, xn_scr, acc_scr, halo_scr):
    TM = TM_FFN

    @pl.when(pl.program_id(0) % tiles_per_seq == 0)
    def _():
        halo_scr[...] = jnp.zeros(halo_scr.shape, F32)

    x1 = (x_ref[...]
          + jnp.dot(od_ref[...], wo_ref[0:DIFF_WIDTH, :], preferred_element_type=F32)
          + jnp.dot(or_ref[...], wo_ref[DIFF_WIDTH:, :], preferred_element_type=F32))
    acc_scr[...] = x1
    xn_scr[...] = _rms(x1, g2_ref[...]).astype(BF16)

    row = lax.broadcasted_iota(jnp.int32, (TM, TF), 0)

    def chunk(c, carry):
        xn = xn_scr[...]
        a = jnp.dot(xn, wa_ref[c], preferred_element_type=F32)
        b = jnp.dot(xn, wb_ref[c], preferred_element_type=F32)
        halo = halo_scr[c]
        halo_scr[c] = a[TM - 8:TM, :]
        a1 = jnp.where(row == 0, halo[7:8, :], pltpu.roll(a, 1, 0))
        a2 = pltpu.roll(a, 2, 0)
        a2 = jnp.where(row == 0, halo[6:7, :], jnp.where(row == 1, halo[7:8, :], a2))
        cw = cw_ref[c]
        u = cw[0:1, :] * a2 + cw[1:2, :] * a1 + cw[2:3, :] * a + cb_ref[c]
        gelu = 0.5 * u * (1.0 + jnp.tanh(0.7978845608028654 * (u + 0.044715 * (u * u * u))))
        acc_scr[...] += jnp.dot((gelu * b).astype(BF16), wd_ref[c], preferred_element_type=F32)
        return carry

    lax.fori_loop(0, NF, chunk, 0)
    o_ref[...] = _rms(acc_scr[...], gf_ref[...])


def _ffn(x2, o_diff, o_ret, wo, g2, wa, wb, cw, cb, wd, gf, seq):
    n = x2.shape[0]
    TM = TM_FFN
    const = lambda shape: pl.BlockSpec(shape, lambda i: (0,) * len(shape),
                                       pipeline_mode=pl.Buffered(1))
    return pl.pallas_call(
        functools.partial(_ffn_kernel, seq // TM),
        grid=(n // TM,),
        in_specs=[
            pl.BlockSpec((TM, D_MODEL), lambda i: (i, 0)),
            pl.BlockSpec((TM, DIFF_WIDTH), lambda i: (i, 0)),
            pl.BlockSpec((TM, RET_WIDTH), lambda i: (i, 0)),
            const((D_MODEL, D_MODEL)), const((1, D_MODEL)),
            const((NF, D_MODEL, TF)), const((NF, D_MODEL, TF)),
            const((NF, 3, TF)), const((NF, 1, TF)),
            const((NF, TF, D_MODEL)), const((1, D_MODEL)),
        ],
        out_specs=pl.BlockSpec((TM, D_MODEL), lambda i: (i, 0)),
        out_shape=jax.ShapeDtypeStruct((n, D_MODEL), F32),
        scratch_shapes=[
            pltpu.VMEM((TM, D_MODEL), BF16),
            pltpu.VMEM((TM, D_MODEL), F32),
            pltpu.VMEM((NF, 8, TF), F32),
        ],
        compiler_params=pltpu.CompilerParams(
            dimension_semantics=("arbitrary",), vmem_limit_bytes=VMEM_LIMIT),
        name="outproj_ffn",
    )(x2, o_diff, o_ret, wo, g2, wa, wb, cw, cb, wd, gf)


def kernel(x, norm_mix_g, w_in, lambda_q1, lambda_k1, lambda_q2, lambda_k2, diff_subln_g,
           ret_gn_g, w_out, norm_ffn_g, w_up, conv_w, conv_b, w_down, final_norm_g):
    batch, seq, _ = x.shape
    x2 = x.reshape(batch * seq, D_MODEL)

    proj = _inproj(x2, norm_mix_g, w_in[0].astype(BF16))
    o_diff = _diff_attention(proj, batch, seq, lambda_q1, lambda_k1, lambda_q2, lambda_k2,
                             diff_subln_g)
    o_ret = _retention(proj, batch, seq, ret_gn_g)

    wa = w_up[0][:, :D_FF].astype(BF16).reshape(D_MODEL, NF, TF).transpose(1, 0, 2)
    wb = w_up[0][:, D_FF:].astype(BF16).reshape(D_MODEL, NF, TF).transpose(1, 0, 2)
    cw = conv_w[0].reshape(3, NF, TF).transpose(1, 0, 2)
    cb = conv_b[0].reshape(NF, 1, TF)
    wd = w_down[0].astype(BF16).reshape(NF, TF, D_MODEL)
    y = _ffn(x2, o_diff, o_ret, w_out[0].astype(BF16), norm_ffn_g, wa, wb, cw, cb, wd,
             final_norm_g.reshape(1, D_MODEL), seq)
    return y.reshape(batch, seq, D_MODEL)
```

```python
import functools
import math

import jax
import jax.numpy as jnp
import numpy as np
from jax import lax
from jax.experimental import pallas as pl
from jax.experimental.pallas import tpu as pltpu

D_MODEL = 1024
CHUNK = 64
DIFF_HEADS = 4
DIFF_HEAD_DIM = 64
DIFF_V_DIM = 128
DIFF_WIDTH = 512
RET_HEADS = 4
RET_KEY_DIM = 64
RET_V_DIM = 128
RET_WIDTH = 512
D_FF = 2816
IN_WIDTH = 3072
EPS = 1e-6
LAM_INIT = 0.8 - 0.6 * math.exp(-0.3 * 0)
LOG2E = math.log2(math.e)
MASKED = -1e30

COL_DQ, COL_DK, COL_DV = 0, 4, 8
COLB_RQ, COLB_RK = 6, 7
COLB_RV, COLB_RG = 4, 5

TM_IN = 512
TQ_ATT = 512
TK_ATT = 512
T_RET = 256
TM_FFN = 512
TF = 256
NF = D_FF // TF
VMEM_LIMIT = 56 * 1024 * 1024

BF16 = jnp.bfloat16
F32 = jnp.float32


def _rms(x, g):
    ms = jnp.mean(x * x, axis=-1, keepdims=True)
    return x * lax.rsqrt(ms + EPS) * g


def _inproj_kernel(x_ref, g_ref, w_ref, o_ref):
    h = _rms(x_ref[...], g_ref[...]).astype(BF16)
    o_ref[...] = jnp.dot(h, w_ref[...], preferred_element_type=F32).astype(o_ref.dtype)


def _inproj(x2, g, w_bf16):
    n = x2.shape[0]
    return pl.pallas_call(
        _inproj_kernel,
        grid=(n // TM_IN,),
        in_specs=[
            pl.BlockSpec((TM_IN, D_MODEL), lambda i: (i, 0)),
            pl.BlockSpec((1, D_MODEL), lambda i: (0, 0)),
            pl.BlockSpec((D_MODEL, IN_WIDTH), lambda i: (0, 0), pipeline_mode=pl.Buffered(1)),
        ],
        out_specs=pl.BlockSpec((TM_IN, IN_WIDTH), lambda i: (i, 0)),
        out_shape=jax.ShapeDtypeStruct((n, IN_WIDTH), BF16),
        compiler_params=pltpu.CompilerParams(
            dimension_semantics=("arbitrary",), vmem_limit_bytes=VMEM_LIMIT),
        name="inproj",
    )(x2, g, w_bf16)


def _attn_kernel(q_ref, k_ref, v_ref, postab_ref, ctab_ref, corr_ref,
                 lq1_ref, lk1_ref, lq2_ref, lk2_ref, g_ref, o_ref,
                 kaug_scr, vt_scr, qt_scr, s_scr, m_scr, l_scr, acc_scr):
    TQ, TK = TQ_ATT, TK_ATT
    i = pl.program_id(2)
    seq = k_ref.shape[0]

    @pl.when(i == 0)
    def _():
        lane = lax.broadcasted_iota(jnp.int32, (TK, 128), 1)
        for c in range(seq // TK):
            rows = slice(c * TK, (c + 1) * TK)
            kc = k_ref[rows, :]
            pt = postab_ref[rows, :]
            kaug_scr[0, rows, :] = jnp.where(lane < DIFF_HEAD_DIM, kc, pt)
            kaug_scr[1, rows, :] = jnp.where(lane >= DIFF_HEAD_DIM, kc, pt)
            vt_scr[c] = v_ref[rows, :].T

    lane = lax.broadcasted_iota(jnp.int32, (TQ, 128), 1)
    q = q_ref[...].astype(F32) * (DIFF_HEAD_DIM ** -0.5 * LOG2E)
    ctab = ctab_ref[0]
    qt_scr[0] = jnp.where(lane < DIFF_HEAD_DIM, q, ctab).T.astype(BF16)
    qt_scr[1] = jnp.where(lane >= DIFF_HEAD_DIM, q, ctab).T.astype(BF16)
    m_scr[...] = jnp.full(m_scr.shape, MASKED, F32)
    l_scr[...] = jnp.zeros(l_scr.shape, F32)
    acc_scr[...] = jnp.zeros(acc_scr.shape, F32)

    def scores(j, slot):
        start = pl.multiple_of(j * TK, TK)
        for half in range(2):
            s_scr[slot, half] = jnp.dot(kaug_scr[half, pl.ds(start, TK), :], qt_scr[half],
                                        preferred_element_type=F32)

    def consume(j, slot, corr):
        vt = vt_scr[j]
        for half in range(2):
            s = s_scr[slot, half]
            if corr is not None:
                s = s + corr
            m_prev = m_scr[half]
            m_new = jnp.maximum(m_prev, jnp.max(s, axis=0, keepdims=True))
            alpha = jnp.exp2(m_prev - m_new)
            p = jnp.exp2(s - m_new)
            l_scr[half] = alpha * l_scr[half] + jnp.sum(p, axis=0, keepdims=True)
            acc_scr[half] = alpha * acc_scr[half] + jnp.dot(
                vt, p.astype(BF16), preferred_element_type=F32)
            m_scr[half] = m_new

    scores(0, 0)

    def tile_pair(t, carry):
        j = 2 * t
        scores(j + 1, 1)
        consume(j, 0, None)
        scores(j + 2, 0)
        consume(j + 1, 1, None)
        return carry

    lax.fori_loop(0, i // 2, tile_pair, 0)

    @pl.when(i % 2 == 0)
    def _():
        consume(i, 0, corr_ref[0])

    @pl.when(i % 2 == 1)
    def _():
        scores(i, 1)
        consume(i - 1, 0, None)
        consume(i, 1, corr_ref[0])

    lam = (jnp.exp(jnp.sum(lq1_ref[...] * lk1_ref[...], axis=1, keepdims=True))
           - jnp.exp(jnp.sum(lq2_ref[...] * lk2_ref[...], axis=1, keepdims=True))
           + LAM_INIT)
    o_t = acc_scr[0] * (1.0 / l_scr[0]) - lam * (acc_scr[1] * (1.0 / l_scr[1]))
    ms = jnp.mean(o_t * o_t, axis=0, keepdims=True)
    y = (o_t * lax.rsqrt(ms + EPS)).T
    o_ref[...] = (y * g_ref[...] * (1.0 - LAM_INIT)).astype(o_ref.dtype)


def _bf16_pieces(x, n):
    x = np.asarray(x, np.float32)
    pieces = []
    for _ in range(n):
        p = x.astype(BF16).astype(np.float32)
        pieces.append(p)
        x = x - p
    return pieces


def _attn_tables(seq):
    TQ, TK = TQ_ATT, TK_ATT
    slopes = (np.exp2(-8.0 * np.arange(1, DIFF_HEADS + 1, dtype=np.float64) / DIFF_HEADS)
              * LOG2E).astype(np.float32)
    c1, c2, c3 = _bf16_pieces(slopes, 3)
    ctab = np.zeros((DIFF_HEADS, 1, 128), np.float32)
    pos = np.arange(seq)
    p_hi, p_lo = (pos // 64) * 64, pos % 64
    postab = np.zeros((seq, 128), np.float32)
    for base in (0, DIFF_HEAD_DIM):
        for n, (cp, pp) in enumerate(((c1, p_hi), (c2, p_hi), (c3, p_hi),
                                      (c1, p_lo), (c2, p_lo), (c3, p_lo))):
            ctab[:, 0, base + n] = cp
            postab[:, base + n] = pp
    c = np.arange(TK)[:, None]
    r = np.arange(TQ)[None, :]
    allowed = (c // CHUNK) <= (r // CHUNK)
    ahead = np.where(c > r, 2.0 * (r - c), 0.0)
    corr = np.where(allowed[None], slopes.astype(np.float64)[:, None, None] * ahead[None], MASKED)
    return jnp.asarray(postab, BF16), jnp.asarray(ctab, F32), jnp.asarray(corr, F32)


def _diff_attention(proj, batch, seq, lq1, lk1, lq2, lk2, subln_g):
    TQ, TK = TQ_ATT, TK_ATT
    assert TQ == TK and seq % TK == 0
    nq = seq // TQ
    postab, ctab, corr = _attn_tables(seq)
    small = lambda shape: pl.BlockSpec(shape, lambda b, h, i: (0,) * len(shape))
    return pl.pallas_call(
        _attn_kernel,
        grid=(batch, DIFF_HEADS, nq),
        in_specs=[
            pl.BlockSpec((TQ, 128), lambda b, h, i: (b * nq + i, COL_DQ + h)),
            pl.BlockSpec((seq, 128), lambda b, h, i: (b, COL_DK + h)),
            pl.BlockSpec((seq, 128), lambda b, h, i: (b, COL_DV + h)),
            small((seq, 128)),
            pl.BlockSpec((1, 1, 128), lambda b, h, i: (h, 0, 0)),
            pl.BlockSpec((1, TK, TQ), lambda b, h, i: (h, 0, 0)),
            small((1, DIFF_HEAD_DIM)), small((1, DIFF_HEAD_DIM)),
            small((1, DIFF_HEAD_DIM)), small((1, DIFF_HEAD_DIM)),
            small((1, DIFF_V_DIM)),
        ],
        out_specs=pl.BlockSpec((TQ, DIFF_V_DIM), lambda b, h, i: (b * nq + i, h)),
        out_shape=jax.ShapeDtypeStruct((batch * seq, DIFF_WIDTH), BF16),
        scratch_shapes=[
            pltpu.VMEM((2, seq, 128), BF16),
            pltpu.VMEM((seq // TK, 128, TK), BF16),
            pltpu.VMEM((2, 128, TQ), BF16),
            pltpu.VMEM((2, 2, TK, TQ), F32),
            pltpu.VMEM((2, 1, TQ), F32),
            pltpu.VMEM((2, 1, TQ), F32),
            pltpu.VMEM((2, DIFF_V_DIM, TQ), F32),
        ],
        compiler_params=pltpu.CompilerParams(
            dimension_semantics=("arbitrary", "arbitrary", "arbitrary"),
            vmem_limit_bytes=VMEM_LIMIT),
        name="diff_attn",
    )(proj, proj, proj, postab, ctab, corr, lq1, lk1, lq2, lk2, subln_g)


def _ret_kernel(q_ref, k_ref, v_ref, gate_ref, dmat_ref, qdec_ref, kdec_ref, cdec_ref,
                gn_ref, o_ref, state_scr):
    T = T_RET

    @pl.when(pl.program_id(1) == 0)
    def _():
        state_scr[...] = jnp.zeros(state_scr.shape, F32)

    lane = lax.broadcasted_iota(jnp.int32, (T, 128), 1)
    for h in range(RET_HEADS):
        pair, half = divmod(h, 2)
        qp = q_ref[:, 128 * pair:128 * (pair + 1)]
        kp = k_ref[:, 128 * pair:128 * (pair + 1)]
        in_head = (lane >= RET_KEY_DIM) if half else (lane < RET_KEY_DIM)
        km = jnp.where(in_head, kp, jnp.zeros_like(kp))
        v = v_ref[:, 128 * h:128 * (h + 1)]
        s = lax.dot_general(qp, km, (((1,), (1,)), ((), ())), preferred_element_type=F32)
        s = s * dmat_ref[h]
        o = jnp.dot(s.astype(BF16), v, preferred_element_type=F32)
        state = state_scr[h]
        o = o + qdec_ref[h] * jnp.dot(qp, state.astype(BF16), preferred_element_type=F32)
        vd = (v.astype(F32) * kdec_ref[h]).astype(BF16)
        state_scr[h] = cdec_ref[h] * state + lax.dot_general(
            km, vd, (((0,), (0,)), ((), ())), preferred_element_type=F32)

        mu = jnp.mean(o, axis=-1, keepdims=True)
        d = o - mu
        var = jnp.mean(d * d, axis=-1, keepdims=True)
        y = d * lax.rsqrt(var + EPS) * gn_ref[:, 128 * h:128 * (h + 1)]
        gate = gate_ref[:, 128 * h:128 * (h + 1)].astype(F32)
        y = y * (gate * (1.0 / (1.0 + jnp.exp(-gate))))
        o_ref[:, 128 * h:128 * (h + 1)] = y.astype(o_ref.dtype)


def _ret_tables():
    T = T_RET
    log_g = np.log1p(-np.exp2(-5.0 - np.arange(RET_HEADS, dtype=np.float64)))
    n = np.arange(T, dtype=np.float64)
    kscale = RET_KEY_DIM ** -0.5
    allowed = (np.arange(T)[None, :] // CHUNK) <= (np.arange(T)[:, None] // CHUNK)
    dmat = np.where(allowed[None],
                    np.exp(log_g[:, None, None] * np.abs(n[:, None] - n[None, :])[None]), 0.0) * kscale
    qdec = np.exp(log_g[:, None] * (n + 1.0)[None]) * kscale
    kdec = np.exp(log_g[:, None] * (T - 1.0 - n)[None])
    cdec = np.exp(log_g * T)
    qdec = np.broadcast_to(qdec[:, :, None], (RET_HEADS, T, 128))
    kdec = np.broadcast_to(kdec[:, :, None], (RET_HEADS, T, 128))
    cdec = np.broadcast_to(cdec[:, None, None], (RET_HEADS, 1, 128))
    return tuple(jnp.asarray(a, F32) for a in (dmat, qdec, kdec, cdec))


def _retention(proj, batch, seq, gn_g):
    T = T_RET
    nt = seq // T
    dmat, qdec, kdec, cdec = _ret_tables()
    const = lambda shape: pl.BlockSpec(shape, lambda b, t: (0,) * len(shape))
    return pl.pallas_call(
        _ret_kernel,
        grid=(batch, nt),
        in_specs=[
            pl.BlockSpec((T, 256), lambda b, t: (b * nt + t, COLB_RQ)),
            pl.BlockSpec((T, 256), lambda b, t: (b * nt + t, COLB_RK)),
            pl.BlockSpec((T, 512), lambda b, t: (b * nt + t, COLB_RV)),
            pl.BlockSpec((T, 512), lambda b, t: (b * nt + t, COLB_RG)),
            const((RET_HEADS, T, T)), const((RET_HEADS, T, 128)),
            const((RET_HEADS, T, 128)), const((RET_HEADS, 1, 128)),
            const((1, RET_WIDTH)),
        ],
        out_specs=pl.BlockSpec((T, RET_WIDTH), lambda b, t: (b * nt + t, 0)),
        out_shape=jax.ShapeDtypeStruct((batch * seq, RET_WIDTH), BF16),
        scratch_shapes=[pltpu.VMEM((RET_HEADS, 128, 128), F32)],
        compiler_params=pltpu.CompilerParams(
            dimension_semantics=("arbitrary", "arbitrary"), vmem_limit_bytes=VMEM_LIMIT),
        name="retention",
    )(proj, proj, proj, proj, dmat, qdec, kdec, cdec, gn_g)


def _ffn_kernel(tiles_per_seq, x_ref, od_ref, or_ref, wo_ref, g2_ref, wa_ref, wb_ref,
                cw_ref, cb_ref, wd_ref, gf_ref, o_ref, xn_scr, acc_scr, halo_scr):
    TM = TM_FFN

    @pl.when(pl.program_id(0) % tiles_per_seq == 0)
    def _():
        halo_scr[...] = jnp.zeros(halo_scr.shape, F32)

    x1 = (x_ref[...]
          + jnp.dot(od_ref[...], wo_ref[0:DIFF_WIDTH, :], preferred_element_type=F32)
          + jnp.dot(or_ref[...], wo_ref[DIFF_WIDTH:, :], preferred_element_type=F32))
    acc_scr[...] = x1
    xn_scr[...] = _rms(x1, g2_ref[...]).astype(BF16)

    row = lax.broadcasted_iota(jnp.int32, (TM, TF), 0)

    def chunk(c, carry):
        xn = xn_scr[...]
        a = jnp.dot(xn, wa_ref[c], preferred_element_type=F32)
        b = jnp.dot(xn, wb_ref[c], preferred_element_type=F32)
        halo = halo_scr[c]
        halo_scr[c] = a[TM - 8:TM, :]
        a1 = jnp.where(row == 0, halo[7:8, :], pltpu.roll(a, 1, 0))
        a2 = pltpu.roll(a, 2, 0)
        a2 = jnp.where(row == 0, halo[6:7, :], jnp.where(row == 1, halo[7:8, :], a2))
        cw = cw_ref[c]
        u = cw[0:1, :] * a2 + cw[1:2, :] * a1 + cw[2:3, :] * a + cb_ref[c]
        gelu = 0.5 * u * (1.0 + jnp.tanh(0.7978845608028654 * (u + 0.044715 * (u * u * u))))
        acc_scr[...] += jnp.dot((gelu * b).astype(BF16), wd_ref[c], preferred_element_type=F32)
        return carry

    lax.fori_loop(0, NF, chunk, 0)
    o_ref[...] = _rms(acc_scr[...], gf_ref[...])


def _ffn(x2, o_diff, o_ret, wo, g2, wa, wb, cw, cb, wd, gf, seq):
    n = x2.shape[0]
    TM = TM_FFN
    const = lambda shape: pl.BlockSpec(shape, lambda i: (0,) * len(shape),
                                       pipeline_mode=pl.Buffered(1))
    return pl.pallas_call(
        functools.partial(_ffn_kernel, seq // TM),
        grid=(n // TM,),
        in_specs=[
            pl.BlockSpec((TM, D_MODEL), lambda i: (i, 0)),
            pl.BlockSpec((TM, DIFF_WIDTH), lambda i: (i, 0)),
            pl.BlockSpec((TM, RET_WIDTH), lambda i: (i, 0)),
            const((D_MODEL, D_MODEL)), const((1, D_MODEL)),
            const((NF, D_MODEL, TF)), const((NF, D_MODEL, TF)),
            const((NF, 3, TF)), const((NF, 1, TF)),
            const((NF, TF, D_MODEL)), const((1, D_MODEL)),
        ],
        out_specs=pl.BlockSpec((TM, D_MODEL), lambda i: (i, 0)),
        out_shape=jax.ShapeDtypeStruct((n, D_MODEL), F32),
        scratch_shapes=[
            pltpu.VMEM((TM, D_MODEL), BF16),
            pltpu.VMEM((TM, D_MODEL), F32),
            pltpu.VMEM((NF, 8, TF), F32),
        ],
        compiler_params=pltpu.CompilerParams(
            dimension_semantics=("arbitrary",), vmem_limit_bytes=VMEM_LIMIT),
        name="outproj_ffn",
    )(x2, o_diff, o_ret, wo, g2, wa, wb, cw, cb, wd, gf)


def kernel(x, norm_mix_g, w_in, lambda_q1, lambda_k1, lambda_q2, lambda_k2, diff_subln_g,
           ret_gn_g, w_out, norm_ffn_g, w_up, conv_w, conv_b, w_down, final_norm_g):
    batch, seq, _ = x.shape
    x2 = x.reshape(batch * seq, D_MODEL)

    proj = _inproj(x2, norm_mix_g, w_in[0].astype(BF16))
    o_diff = _diff_attention(proj, batch, seq, lambda_q1, lambda_k1, lambda_q2, lambda_k2,
                             diff_subln_g)
    o_ret = _retention(proj, batch, seq, ret_gn_g)

    wa = w_up[0][:, :D_FF].astype(BF16).reshape(D_MODEL, NF, TF).transpose(1, 0, 2)
    wb = w_up[0][:, D_FF:].astype(BF16).reshape(D_MODEL, NF, TF).transpose(1, 0, 2)
    cw = conv_w[0].reshape(3, NF, TF).transpose(1, 0, 2)
    cb = conv_b[0].reshape(NF, 1, TF)
    wd = w_down[0].astype(BF16).reshape(NF, TF, D_MODEL)
    y = _ffn(x2, o_diff, o_ret, w_out[0].astype(BF16), norm_ffn_g, wa, wb, cw, cb, wd,
             final_norm_g.reshape(1, D_MODEL), seq)
    return y.reshape(batch, seq, D_MODEL)
```

```python
import functools
import math

import jax
import jax.numpy as jnp
import numpy as np
from jax import lax
from jax.experimental import pallas as pl
from jax.experimental.pallas import tpu as pltpu

D_MODEL = 1024
CHUNK = 64
DIFF_HEADS = 4
DIFF_HEAD_DIM = 64
DIFF_V_DIM = 128
DIFF_WIDTH = 512
RET_HEADS = 4
RET_KEY_DIM = 64
RET_V_DIM = 128
RET_WIDTH = 512
D_FF = 2816
IN_WIDTH = 3072
EPS = 1e-6
LAM_INIT = 0.8 - 0.6 * math.exp(-0.3 * 0)
LOG2E = math.log2(math.e)
MASKED = -1e30

COL_DQ, COL_DK, COL_DV = 0, 4, 8
COLB_RQ, COLB_RK = 6, 7
COLB_RV, COLB_RG = 4, 5

TM_IN = 512
TQ_ATT = 512
TK_ATT = 512
V_ROWS = DIFF_V_DIM + 16
T_RET = 256
TM_FFN = 512
TF = 256
NF = D_FF // TF
VMEM_LIMIT = 56 * 1024 * 1024

BF16 = jnp.bfloat16
F32 = jnp.float32


def _rms(x, g):
    ms = jnp.mean(x * x, axis=-1, keepdims=True)
    return x * lax.rsqrt(ms + EPS) * g


def _inproj_kernel(x_ref, g_ref, w_ref, o_ref):
    h = _rms(x_ref[...], g_ref[...]).astype(BF16)
    o_ref[...] = jnp.dot(h, w_ref[...], preferred_element_type=F32).astype(o_ref.dtype)


def _inproj(x2, g, w_bf16):
    n = x2.shape[0]
    return pl.pallas_call(
        _inproj_kernel,
        grid=(n // TM_IN,),
        in_specs=[
            pl.BlockSpec((TM_IN, D_MODEL), lambda i: (i, 0)),
            pl.BlockSpec((1, D_MODEL), lambda i: (0, 0)),
            pl.BlockSpec((D_MODEL, IN_WIDTH), lambda i: (0, 0), pipeline_mode=pl.Buffered(1)),
        ],
        out_specs=pl.BlockSpec((TM_IN, IN_WIDTH), lambda i: (i, 0)),
        out_shape=jax.ShapeDtypeStruct((n, IN_WIDTH), BF16),
        compiler_params=pltpu.CompilerParams(
            dimension_semantics=("arbitrary",), vmem_limit_bytes=VMEM_LIMIT),
        name="inproj",
    )(x2, g, w_bf16)


def _attn_kernel(q_ref, k_ref, v_ref, postab_ref, ctab_ref, corr_ref,
                 lq1_ref, lk1_ref, lq2_ref, lk2_ref, g_ref, o_ref,
                 kaug_scr, vt_scr, qt_scr, s_scr, m_scr, acc_scr):
    TQ, TK = TQ_ATT, TK_ATT
    DV = DIFF_V_DIM
    i = pl.program_id(2)
    seq = k_ref.shape[0]

    @pl.when(i == 0)
    def _():
        lane = lax.broadcasted_iota(jnp.int32, (TK, 128), 1)
        for c in range(seq // TK):
            rows = slice(c * TK, (c + 1) * TK)
            kc = k_ref[rows, :]
            pt = postab_ref[rows, :]
            kaug_scr[0, rows, :] = jnp.where(lane < DIFF_HEAD_DIM, kc, pt)
            kaug_scr[1, rows, :] = jnp.where(lane >= DIFF_HEAD_DIM, kc, pt)
            vt_scr[c, 0:DV, :] = v_ref[rows, :].T
            vt_scr[c, DV:, :] = jnp.ones((V_ROWS - DV, TK), BF16)

    lane = lax.broadcasted_iota(jnp.int32, (TQ, 128), 1)
    q = q_ref[...].astype(F32) * (DIFF_HEAD_DIM ** -0.5 * LOG2E)
    ctab = ctab_ref[0]
    qt_scr[0] = jnp.where(lane < DIFF_HEAD_DIM, q, ctab).T.astype(BF16)
    qt_scr[1] = jnp.where(lane >= DIFF_HEAD_DIM, q, ctab).T.astype(BF16)
    m_scr[...] = jnp.full(m_scr.shape, MASKED, F32)
    acc_scr[...] = jnp.zeros(acc_scr.shape, F32)

    def scores(j, slot):
        start = pl.multiple_of(j * TK, TK)
        for half in range(2):
            s_scr[slot, half] = jnp.dot(kaug_scr[half, pl.ds(start, TK), :], qt_scr[half],
                                        preferred_element_type=F32)

    def consume(j, slot, corr):
        vt = vt_scr[j]
        for half in range(2):
            s = s_scr[slot, half]
            if corr is not None:
                s = s + corr
            m_prev = m_scr[half]
            m_new = jnp.maximum(m_prev, jnp.max(s, axis=0, keepdims=True))
            alpha = jnp.exp2(m_prev - m_new)
            p = jnp.exp2(s - m_new).astype(BF16)
            acc_scr[half] = alpha * acc_scr[half] + jnp.dot(
                vt, p, preferred_element_type=F32)
            m_scr[half] = m_new

    scores(0, 0)

    def tile_pair(t, carry):
        j = 2 * t
        scores(j + 1, 1)
        consume(j, 0, None)
        scores(j + 2, 0)
        consume(j + 1, 1, None)
        return carry

    lax.fori_loop(0, i // 2, tile_pair, 0)

    @pl.when(i % 2 == 0)
    def _():
        consume(i, 0, corr_ref[0])

    @pl.when(i % 2 == 1)
    def _():
        scores(i, 1)
        consume(i - 1, 0, None)
        consume(i, 1, corr_ref[0])

    lam = (jnp.exp(jnp.sum(lq1_ref[...] * lk1_ref[...], axis=1, keepdims=True))
           - jnp.exp(jnp.sum(lq2_ref[...] * lk2_ref[...], axis=1, keepdims=True))
           + LAM_INIT)
    inv_l0 = 1.0 / acc_scr[0, DV:DV + 1, :]
    inv_l1 = 1.0 / acc_scr[1, DV:DV + 1, :]
    o_t = acc_scr[0, 0:DV, :] * inv_l0 - lam * (acc_scr[1, 0:DV, :] * inv_l1)
    ms = jnp.mean(o_t * o_t, axis=0, keepdims=True)
    y = (o_t * lax.rsqrt(ms + EPS)).T
    o_ref[...] = (y * g_ref[...] * (1.0 - LAM_INIT)).astype(o_ref.dtype)


def _bf16_pieces(x, n):
    x = np.asarray(x, np.float32)
    pieces = []
    for _ in range(n):
        p = x.astype(BF16).astype(np.float32)
        pieces.append(p)
        x = x - p
    return pieces


def _attn_tables(seq):
    TQ, TK = TQ_ATT, TK_ATT
    slopes = (np.exp2(-8.0 * np.arange(1, DIFF_HEADS + 1, dtype=np.float64) / DIFF_HEADS)
              * LOG2E).astype(np.float32)
    c1, c2, c3 = _bf16_pieces(slopes, 3)
    ctab = np.zeros((DIFF_HEADS, 1, 128), np.float32)
    pos = np.arange(seq)
    p_hi, p_lo = (pos // 64) * 64, pos % 64
    postab = np.zeros((seq, 128), np.float32)
    for base in (0, DIFF_HEAD_DIM):
        for n, (cp, pp) in enumerate(((c1, p_hi), (c2, p_hi), (c3, p_hi),
                                      (c1, p_lo), (c2, p_lo), (c3, p_lo))):
            ctab[:, 0, base + n] = cp
            postab[:, base + n] = pp
    c = np.arange(TK)[:, None]
    r = np.arange(TQ)[None, :]
    allowed = (c // CHUNK) <= (r // CHUNK)
    ahead = np.where(c > r, 2.0 * (r - c), 0.0)
    corr = np.where(allowed[None], slopes.astype(np.float64)[:, None, None] * ahead[None], MASKED)
    return jnp.asarray(postab, BF16), jnp.asarray(ctab, F32), jnp.asarray(corr, F32)


def _diff_attention(proj, batch, seq, lq1, lk1, lq2, lk2, subln_g):
    TQ, TK = TQ_ATT, TK_ATT
    assert TQ == TK and seq % TK == 0
    nq = seq // TQ
    postab, ctab, corr = _attn_tables(seq)
    small = lambda shape: pl.BlockSpec(shape, lambda b, h, i: (0,) * len(shape))
    return pl.pallas_call(
        _attn_kernel,
        grid=(batch, DIFF_HEADS, nq),
        in_specs=[
            pl.BlockSpec((TQ, 128), lambda b, h, i: (b * nq + i, COL_DQ + h)),
            pl.BlockSpec((seq, 128), lambda b, h, i: (b, COL_DK + h)),
            pl.BlockSpec((seq, 128), lambda b, h, i: (b, COL_DV + h)),
            small((seq, 128)),
            pl.BlockSpec((1, 1, 128), lambda b, h, i: (h, 0, 0)),
            pl.BlockSpec((1, TK, TQ), lambda b, h, i: (h, 0, 0)),
            small((1, DIFF_HEAD_DIM)), small((1, DIFF_HEAD_DIM)),
            small((1, DIFF_HEAD_DIM)), small((1, DIFF_HEAD_DIM)),
            small((1, DIFF_V_DIM)),
        ],
        out_specs=pl.BlockSpec((TQ, DIFF_V_DIM), lambda b, h, i: (b * nq + i, h)),
        out_shape=jax.ShapeDtypeStruct((batch * seq, DIFF_WIDTH), BF16),
        scratch_shapes=[
            pltpu.VMEM((2, seq, 128), BF16),
            pltpu.VMEM((seq // TK, V_ROWS, TK), BF16),
            pltpu.VMEM((2, 128, TQ), BF16),
            pltpu.VMEM((2, 2, TK, TQ), F32),
            pltpu.VMEM((2, 1, TQ), F32),
            pltpu.VMEM((2, V_ROWS, TQ), F32),
        ],
        compiler_params=pltpu.CompilerParams(
            dimension_semantics=("arbitrary", "arbitrary", "arbitrary"),
            vmem_limit_bytes=VMEM_LIMIT),
        name="diff_attn",
    )(proj, proj, proj, postab, ctab, corr, lq1, lk1, lq2, lk2, subln_g)


def _ret_kernel(q_ref, k_ref, v_ref, gate_ref, dmat_ref, qdec_ref, kdec_ref, cdec_ref,
                gn_ref, o_ref, state_scr):
    T = T_RET

    @pl.when(pl.program_id(1) == 0)
    def _():
        state_scr[...] = jnp.zeros(state_scr.shape, F32)

    lane = lax.broadcasted_iota(jnp.int32, (T, 128), 1)
    for h in range(RET_HEADS):
        pair, half = divmod(h, 2)
        qp = q_ref[:, 128 * pair:128 * (pair + 1)]
        kp = k_ref[:, 128 * pair:128 * (pair + 1)]
        in_head = (lane >= RET_KEY_DIM) if half else (lane < RET_KEY_DIM)
        km = jnp.where(in_head, kp, jnp.zeros_like(kp))
        v = v_ref[:, 128 * h:128 * (h + 1)]
        s = lax.dot_general(qp, km, (((1,), (1,)), ((), ())), preferred_element_type=F32)
        s = s * dmat_ref[h]
        o = jnp.dot(s.astype(BF16), v, preferred_element_type=F32)
        state = state_scr[h]
        o = o + qdec_ref[h] * jnp.dot(qp, state.astype(BF16), preferred_element_type=F32)
        vd = (v.astype(F32) * kdec_ref[h]).astype(BF16)
        state_scr[h] = cdec_ref[h] * state + lax.dot_general(
            km, vd, (((0,), (0,)), ((), ())), preferred_element_type=F32)

        mu = jnp.mean(o, axis=-1, keepdims=True)
        d = o - mu
        var = jnp.mean(d * d, axis=-1, keepdims=True)
        y = d * lax.rsqrt(var + EPS) * gn_ref[:, 128 * h:128 * (h + 1)]
        gate = gate_ref[:, 128 * h:128 * (h + 1)].astype(F32)
        y = y * (gate * (1.0 / (1.0 + jnp.exp(-gate))))
        o_ref[:, 128 * h:128 * (h + 1)] = y.astype(o_ref.dtype)


def _ret_tables():
    T = T_RET
    log_g = np.log1p(-np.exp2(-5.0 - np.arange(RET_HEADS, dtype=np.float64)))
    n = np.arange(T, dtype=np.float64)
    kscale = RET_KEY_DIM ** -0.5
    allowed = (np.arange(T)[None, :] // CHUNK) <= (np.arange(T)[:, None] // CHUNK)
    dmat = np.where(allowed[None],
                    np.exp(log_g[:, None, None] * np.abs(n[:, None] - n[None, :])[None]), 0.0) * kscale
    qdec = np.exp(log_g[:, None] * (n + 1.0)[None]) * kscale
    kdec = np.exp(log_g[:, None] * (T - 1.0 - n)[None])
    cdec = np.exp(log_g * T)
    qdec = np.broadcast_to(qdec[:, :, None], (RET_HEADS, T, 128))
    kdec = np.broadcast_to(kdec[:, :, None], (RET_HEADS, T, 128))
    cdec = np.broadcast_to(cdec[:, None, None], (RET_HEADS, 1, 128))
    return tuple(jnp.asarray(a, F32) for a in (dmat, qdec, kdec, cdec))


def _retention(proj, batch, seq, gn_g):
    T = T_RET
    nt = seq // T
    dmat, qdec, kdec, cdec = _ret_tables()
    const = lambda shape: pl.BlockSpec(shape, lambda b, t: (0,) * len(shape))
    return pl.pallas_call(
        _ret_kernel,
        grid=(batch, nt),
        in_specs=[
            pl.BlockSpec((T, 256), lambda b, t: (b * nt + t, COLB_RQ)),
            pl.BlockSpec((T, 256), lambda b, t: (b * nt + t, COLB_RK)),
            pl.BlockSpec((T, 512), lambda b, t: (b * nt + t, COLB_RV)),
            pl.BlockSpec((T, 512), lambda b, t: (b * nt + t, COLB_RG)),
            const((RET_HEADS, T, T)), const((RET_HEADS, T, 128)),
            const((RET_HEADS, T, 128)), const((RET_HEADS, 1, 128)),
            const((1, RET_WIDTH)),
        ],
        out_specs=pl.BlockSpec((T, RET_WIDTH), lambda b, t: (b * nt + t, 0)),
        out_shape=jax.ShapeDtypeStruct((batch * seq, RET_WIDTH), BF16),
        scratch_shapes=[pltpu.VMEM((RET_HEADS, 128, 128), F32)],
        compiler_params=pltpu.CompilerParams(
            dimension_semantics=("arbitrary", "arbitrary"), vmem_limit_bytes=VMEM_LIMIT),
        name="retention",
    )(proj, proj, proj, proj, dmat, qdec, kdec, cdec, gn_g)


def _ffn_kernel(tiles_per_seq, x_ref, od_ref, or_ref, wo_ref, g2_ref, wa_ref, wb_ref,
                cw_ref, cb_ref, wd_ref, gf_ref, o_ref, xn_scr, h_scr, halo_scr):
    TM = TM_FFN

    @pl.when(pl.program_id(0) % tiles_per_seq == 0)
    def _():
        halo_scr[...] = jnp.zeros(halo_scr.shape, F32)

    x1 = (x_ref[...]
          + jnp.dot(od_ref[...], wo_ref[0:DIFF_WIDTH, :], preferred_element_type=F32)
          + jnp.dot(or_ref[...], wo_ref[DIFF_WIDTH:, :], preferred_element_type=F32))
    o_ref[...] = x1
    xn_scr[...] = _rms(x1, g2_ref[...]).astype(BF16)

    row = lax.broadcasted_iota(jnp.int32, (TM, TF), 0)

    for c in range(NF):
        xn = xn_scr[...]
        a = jnp.dot(xn, wa_ref[c], preferred_element_type=F32)
        b = jnp.dot(xn, wb_ref[c], preferred_element_type=F32)
        halo = halo_scr[c]
        halo_scr[c] = a[TM - 8:TM, :]
        a1 = jnp.where(row == 0, halo[7:8, :], pltpu.roll(a, 1, 0))
        a2 = pltpu.roll(a, 2, 0)
        a2 = jnp.where(row == 0, halo[6:7, :], jnp.where(row == 1, halo[7:8, :], a2))
        cw = cw_ref[c]
        u = cw[0:1, :] * a2 + cw[1:2, :] * a1 + cw[2:3, :] * a + cb_ref[c]
        gelu = 0.5 * u * (1.0 + jnp.tanh(0.7978845608028654 * (u + 0.044715 * (u * u * u))))
        h_scr[:, c * TF:(c + 1) * TF] = (gelu * b).astype(BF16)

    y = o_ref[...] + jnp.dot(h_scr[...], wd_ref[...], preferred_element_type=F32)
    o_ref[...] = _rms(y, gf_ref[...])


def _ffn(x2, o_diff, o_ret, wo, g2, wa, wb, cw, cb, wd, gf, seq):
    n = x2.shape[0]
    TM = TM_FFN
    const = lambda shape: pl.BlockSpec(shape, lambda i: (0,) * len(shape),
                                       pipeline_mode=pl.Buffered(1))
    return pl.pallas_call(
        functools.partial(_ffn_kernel, seq // TM),
        grid=(n // TM,),
        in_specs=[
            pl.BlockSpec((TM, D_MODEL), lambda i: (i, 0)),
            pl.BlockSpec((TM, DIFF_WIDTH), lambda i: (i, 0)),
            pl.BlockSpec((TM, RET_WIDTH), lambda i: (i, 0)),
            const((D_MODEL, D_MODEL)), const((1, D_MODEL)),
            const((NF, D_MODEL, TF)), const((NF, D_MODEL, TF)),
            const((NF, 3, TF)), const((NF, 1, TF)),
            const((D_FF, D_MODEL)), const((1, D_MODEL)),
        ],
        out_specs=pl.BlockSpec((TM, D_MODEL), lambda i: (i, 0)),
        out_shape=jax.ShapeDtypeStruct((n, D_MODEL), F32),
        scratch_shapes=[
            pltpu.VMEM((TM, D_MODEL), BF16),
            pltpu.VMEM((TM, D_FF), BF16),
            pltpu.VMEM((NF, 8, TF), F32),
        ],
        compiler_params=pltpu.CompilerParams(
            dimension_semantics=("arbitrary",), vmem_limit_bytes=VMEM_LIMIT),
        name="outproj_ffn",
    )(x2, o_diff, o_ret, wo, g2, wa, wb, cw, cb, wd, gf)


def kernel(x, norm_mix_g, w_in, lambda_q1, lambda_k1, lambda_q2, lambda_k2, diff_subln_g,
           ret_gn_g, w_out, norm_ffn_g, w_up, conv_w, conv_b, w_down, final_norm_g):
    batch, seq, _ = x.shape
    x2 = x.reshape(batch * seq, D_MODEL)

    proj = _inproj(x2, norm_mix_g, w_in[0].astype(BF16))
    o_diff = _diff_attention(proj, batch, seq, lambda_q1, lambda_k1, lambda_q2, lambda_k2,
                             diff_subln_g)
    o_ret = _retention(proj, batch, seq, ret_gn_g)

    wa = w_up[0][:, :D_FF].astype(BF16).reshape(D_MODEL, NF, TF).transpose(1, 0, 2)
    wb = w_up[0][:, D_FF:].astype(BF16).reshape(D_MODEL, NF, TF).transpose(1, 0, 2)
    cw = conv_w[0].reshape(3, NF, TF).transpose(1, 0, 2)
    cb = conv_b[0].reshape(NF, 1, TF)
    wd = w_down[0].astype(BF16)
    y = _ffn(x2, o_diff, o_ret, w_out[0].astype(BF16), norm_ffn_g, wa, wb, cw, cb, wd,
             final_norm_g.reshape(1, D_MODEL), seq)
    return y.reshape(batch, seq, D_MODEL)
```

```python
import functools
import math

import jax
import jax.numpy as jnp
import numpy as np
from jax import lax
from jax.experimental import pallas as pl
from jax.experimental.pallas import tpu as pltpu

D_MODEL = 1024
CHUNK = 64
DIFF_HEADS = 4
DIFF_HEAD_DIM = 64
DIFF_V_DIM = 128
DIFF_WIDTH = 512
RET_HEADS = 4
RET_KEY_DIM = 64
RET_V_DIM = 128
RET_WIDTH = 512
D_FF = 2816
IN_WIDTH = 3072
EPS = 1e-6
LAM_INIT = 0.8 - 0.6 * math.exp(-0.3 * 0)
LOG2E = math.log2(math.e)
MASKED = -1e30

COL_DQ, COL_DK, COL_DV = 0, 4, 8
COLB_RQ, COLB_RK = 6, 7
COLB_RV, COLB_RG = 4, 5

TM_IN = 512
TQ_ATT = 512
TK_ATT = 512
V_ROWS = DIFF_V_DIM + 16
EXP2_UNDERFLOW = 150.0
BOUND_MARGIN = 2.0
NORM_SLACK = 1.001
T_RET = 256
TM_FFN = 512
TF = 256
NF = D_FF // TF
VMEM_LIMIT = 56 * 1024 * 1024

BF16 = jnp.bfloat16
F32 = jnp.float32


def _rms(x, g):
    ms = jnp.mean(x * x, axis=-1, keepdims=True)
    return x * lax.rsqrt(ms + EPS) * g


def _inproj_kernel(x_ref, g_ref, w_ref, o_ref):
    h = _rms(x_ref[...], g_ref[...]).astype(BF16)
    o_ref[...] = jnp.dot(h, w_ref[...], preferred_element_type=F32).astype(o_ref.dtype)


def _inproj(x2, g, w_bf16):
    n = x2.shape[0]
    return pl.pallas_call(
        _inproj_kernel,
        grid=(n // TM_IN,),
        in_specs=[
            pl.BlockSpec((TM_IN, D_MODEL), lambda i: (i, 0)),
            pl.BlockSpec((1, D_MODEL), lambda i: (0, 0)),
            pl.BlockSpec((D_MODEL, IN_WIDTH), lambda i: (0, 0), pipeline_mode=pl.Buffered(1)),
        ],
        out_specs=pl.BlockSpec((TM_IN, IN_WIDTH), lambda i: (i, 0)),
        out_shape=jax.ShapeDtypeStruct((n, IN_WIDTH), BF16),
        compiler_params=pltpu.CompilerParams(
            dimension_semantics=("arbitrary",), vmem_limit_bytes=VMEM_LIMIT),
        name="inproj",
    )(x2, g, w_bf16)


def _attn_kernel(q_ref, k_ref, v_ref, postab_ref, ctab_ref, cpos_ref, corr_ref,
                 lq1_ref, lk1_ref, lq2_ref, lk2_ref, g_ref, o_ref,
                 kaug_scr, vt_scr, knorm_scr, qt_scr, s_scr, m_scr, acc_scr):
    TQ, TK = TQ_ATT, TK_ATT
    DV, DH = DIFF_V_DIM, DIFF_HEAD_DIM
    i = pl.program_id(2)
    seq = k_ref.shape[0]
    tile_lane = lax.broadcasted_iota(jnp.int32, (1, 128), 1)

    @pl.when(i == 0)
    def _():
        lane = lax.broadcasted_iota(jnp.int32, (TK, 128), 1)
        tile_max = [jnp.zeros((1, 128), F32)] * 2
        prefix_max = [jnp.zeros((1, 128), F32)] * 2
        running = [jnp.zeros((1, 1), F32)] * 2
        for c in range(seq // TK):
            rows = slice(c * TK, (c + 1) * TK)
            kc = k_ref[rows, :]
            pt = postab_ref[rows, :]
            kaug_scr[0, rows, :] = jnp.where(lane < DH, kc, pt)
            kaug_scr[1, rows, :] = jnp.where(lane >= DH, kc, pt)
            vt_scr[c, 0:DV, :] = v_ref[rows, :].T
            vt_scr[c, DV:, :] = jnp.ones((V_ROWS - DV, TK), BF16)
            ksq = kc.astype(F32)
            ksq = ksq * ksq
            for half in range(2):
                in_half = (lane >= DH) if half else (lane < DH)
                n2 = jnp.sum(jnp.where(in_half, ksq, 0.0), axis=1, keepdims=True)
                norm = jnp.sqrt(jnp.max(n2, axis=0, keepdims=True))
                running[half] = jnp.maximum(running[half], norm)
                tile_max[half] = jnp.where(tile_lane == c, norm, tile_max[half])
                prefix_max[half] = jnp.where(tile_lane == c, running[half], prefix_max[half])
        for half in range(2):
            knorm_scr[half] = tile_max[half]
            knorm_scr[2 + half] = prefix_max[half]

    lane = lax.broadcasted_iota(jnp.int32, (TQ, 128), 1)
    q = q_ref[...].astype(F32) * (DH ** -0.5 * LOG2E)
    ctab = ctab_ref[0]
    q_norm = []
    for half in range(2):
        in_half = (lane >= DH) if half else (lane < DH)
        qb = jnp.where(in_half, q, ctab).T.astype(BF16)
        qt_scr[half] = qb
        qf = qb[half * DH:(half + 1) * DH, :].astype(F32)
        q_norm.append(jnp.sqrt(jnp.max(jnp.sum(qf * qf, axis=0, keepdims=True),
                                       axis=1, keepdims=True)))
    m_scr[...] = jnp.full(m_scr.shape, MASKED, F32)
    acc_scr[...] = jnp.zeros(acc_scr.shape, F32)

    slope_end = cpos_ref[0, 0:1, :]
    slope_q0 = cpos_ref[0, 1:2, :] * (i * TQ).astype(F32)
    skip = tile_lane < i
    for half in range(2):
        qn = q_norm[half] * NORM_SLACK
        own_tile = jnp.sum(jnp.where(tile_lane == i, knorm_scr[half], 0.0), axis=1, keepdims=True)
        upper = qn * knorm_scr[2 + half] + slope_end
        lower = slope_q0 - qn * own_tile
        skip = jnp.logical_and(skip, upper + BOUND_MARGIN < lower - EXP2_UNDERFLOW)
    first = jnp.sum(skip.astype(jnp.int32))
    n_past = i - first

    def scores(j, slot):
        start = pl.multiple_of(j * TK, TK)
        for half in range(2):
            s_scr[slot, half] = jnp.dot(kaug_scr[half, pl.ds(start, TK), :], qt_scr[half],
                                        preferred_element_type=F32)

    def consume(j, slot, corr):
        vt = vt_scr[j]
        for half in range(2):
            s = s_scr[slot, half]
            if corr is not None:
                s = s + corr
            m_prev = m_scr[half]
            m_new = jnp.maximum(m_prev, jnp.max(s, axis=0, keepdims=True))
            alpha = jnp.exp2(m_prev - m_new)
            p = jnp.exp2(s - m_new).astype(BF16)
            acc_scr[half] = alpha * acc_scr[half] + jnp.dot(
                vt, p, preferred_element_type=F32)
            m_scr[half] = m_new

    scores(first, 0)

    def tile_pair(t, carry):
        j = first + 2 * t
        scores(j + 1, 1)
        consume(j, 0, None)
        scores(j + 2, 0)
        consume(j + 1, 1, None)
        return carry

    lax.fori_loop(0, n_past // 2, tile_pair, 0)

    @pl.when(n_past % 2 == 0)
    def _():
        consume(i, 0, corr_ref[0])

    @pl.when(n_past % 2 == 1)
    def _():
        scores(i, 1)
        consume(i - 1, 0, None)
        consume(i, 1, corr_ref[0])

    lam = (jnp.exp(jnp.sum(lq1_ref[...] * lk1_ref[...], axis=1, keepdims=True))
           - jnp.exp(jnp.sum(lq2_ref[...] * lk2_ref[...], axis=1, keepdims=True))
           + LAM_INIT)
    inv_l0 = 1.0 / acc_scr[0, DV:DV + 1, :]
    inv_l1 = 1.0 / acc_scr[1, DV:DV + 1, :]
    o_t = acc_scr[0, 0:DV, :] * inv_l0 - lam * (acc_scr[1, 0:DV, :] * inv_l1)
    ms = jnp.mean(o_t * o_t, axis=0, keepdims=True)
    y = (o_t * lax.rsqrt(ms + EPS)).T
    o_ref[...] = (y * g_ref[...] * (1.0 - LAM_INIT)).astype(o_ref.dtype)


def _bf16_pieces(x, n):
    x = np.asarray(x, np.float32)
    pieces = []
    for _ in range(n):
        p = x.astype(BF16).astype(np.float32)
        pieces.append(p)
        x = x - p
    return pieces


def _attn_tables(seq):
    TQ, TK = TQ_ATT, TK_ATT
    slopes = (np.exp2(-8.0 * np.arange(1, DIFF_HEADS + 1, dtype=np.float64) / DIFF_HEADS)
              * LOG2E).astype(np.float32)
    c1, c2, c3 = _bf16_pieces(slopes, 3)
    ctab = np.zeros((DIFF_HEADS, 1, 128), np.float32)
    pos = np.arange(seq)
    p_hi, p_lo = (pos // 64) * 64, pos % 64
    postab = np.zeros((seq, 128), np.float32)
    for base in (0, DIFF_HEAD_DIM):
        for n, (cp, pp) in enumerate(((c1, p_hi), (c2, p_hi), (c3, p_hi),
                                      (c1, p_lo), (c2, p_lo), (c3, p_lo))):
            ctab[:, 0, base + n] = cp
            postab[:, base + n] = pp
    c = np.arange(TK)[:, None]
    r = np.arange(TQ)[None, :]
    allowed = (c // CHUNK) <= (r // CHUNK)
    ahead = np.where(c > r, 2.0 * (r - c), 0.0)
    corr = np.where(allowed[None], slopes.astype(np.float64)[:, None, None] * ahead[None], MASKED)
    cpos = np.zeros((DIFF_HEADS, 2, 128), np.float64)
    cpos[:, 0, :] = slopes.astype(np.float64)[:, None] * ((np.arange(128) + 1) * TK - 1)[None, :]
    cpos[:, 1, :] = slopes.astype(np.float64)[:, None]
    return (jnp.asarray(postab, BF16), jnp.asarray(ctab, F32), jnp.asarray(cpos, F32),
            jnp.asarray(corr, F32))


def _diff_attention(proj, batch, seq, lq1, lk1, lq2, lk2, subln_g):
    TQ, TK = TQ_ATT, TK_ATT
    assert TQ == TK and seq % TK == 0
    nq = seq // TQ
    assert seq // TK <= 128
    postab, ctab, cpos, corr = _attn_tables(seq)
    small = lambda shape: pl.BlockSpec(shape, lambda b, h, i: (0,) * len(shape))
    return pl.pallas_call(
        _attn_kernel,
        grid=(batch, DIFF_HEADS, nq),
        in_specs=[
            pl.BlockSpec((TQ, 128), lambda b, h, i: (b * nq + i, COL_DQ + h)),
            pl.BlockSpec((seq, 128), lambda b, h, i: (b, COL_DK + h)),
            pl.BlockSpec((seq, 128), lambda b, h, i: (b, COL_DV + h)),
            small((seq, 128)),
            pl.BlockSpec((1, 1, 128), lambda b, h, i: (h, 0, 0)),
            pl.BlockSpec((1, 2, 128), lambda b, h, i: (h, 0, 0)),
            pl.BlockSpec((1, TK, TQ), lambda b, h, i: (h, 0, 0)),
            small((1, DIFF_HEAD_DIM)), small((1, DIFF_HEAD_DIM)),
            small((1, DIFF_HEAD_DIM)), small((1, DIFF_HEAD_DIM)),
            small((1, DIFF_V_DIM)),
        ],
        out_specs=pl.BlockSpec((TQ, DIFF_V_DIM), lambda b, h, i: (b * nq + i, h)),
        out_shape=jax.ShapeDtypeStruct((batch * seq, DIFF_WIDTH), BF16),
        scratch_shapes=[
            pltpu.VMEM((2, seq, 128), BF16),
            pltpu.VMEM((seq // TK, V_ROWS, TK), BF16),
            pltpu.VMEM((4, 1, 128), F32),
            pltpu.VMEM((2, 128, TQ), BF16),
            pltpu.VMEM((2, 2, TK, TQ), F32),
            pltpu.VMEM((2, 1, TQ), F32),
            pltpu.VMEM((2, V_ROWS, TQ), F32),
        ],
        compiler_params=pltpu.CompilerParams(
            dimension_semantics=("arbitrary", "arbitrary", "arbitrary"),
            vmem_limit_bytes=VMEM_LIMIT),
        name="diff_attn",
    )(proj, proj, proj, postab, ctab, cpos, corr, lq1, lk1, lq2, lk2, subln_g)


def _ret_kernel(q_ref, k_ref, v_ref, gate_ref, dmat_ref, qdec_ref, kdec_ref, cdec_ref,
                gn_ref, o_ref, state_scr):
    T = T_RET

    @pl.when(pl.program_id(1) == 0)
    def _():
        state_scr[...] = jnp.zeros(state_scr.shape, F32)

    lane = lax.broadcasted_iota(jnp.int32, (T, 128), 1)
    for h in range(RET_HEADS):
        pair, half = divmod(h, 2)
        qp = q_ref[:, 128 * pair:128 * (pair + 1)]
        kp = k_ref[:, 128 * pair:128 * (pair + 1)]
        in_head = (lane >= RET_KEY_DIM) if half else (lane < RET_KEY_DIM)
        km = jnp.where(in_head, kp, jnp.zeros_like(kp))
        v = v_ref[:, 128 * h:128 * (h + 1)]
        s = lax.dot_general(qp, km, (((1,), (1,)), ((), ())), preferred_element_type=F32)
        s = s * dmat_ref[h]
        o = jnp.dot(s.astype(BF16), v, preferred_element_type=F32)
        state = state_scr[h]
        o = o + qdec_ref[h] * jnp.dot(qp, state.astype(BF16), preferred_element_type=F32)
        vd = (v.astype(F32) * kdec_ref[h]).astype(BF16)
        state_scr[h] = cdec_ref[h] * state + lax.dot_general(
            km, vd, (((0,), (0,)), ((), ())), preferred_element_type=F32)

        mu = jnp.mean(o, axis=-1, keepdims=True)
        d = o - mu
        var = jnp.mean(d * d, axis=-1, keepdims=True)
        y = d * lax.rsqrt(var + EPS) * gn_ref[:, 128 * h:128 * (h + 1)]
        gate = gate_ref[:, 128 * h:128 * (h + 1)].astype(F32)
        y = y * (gate * (1.0 / (1.0 + jnp.exp(-gate))))
        o_ref[:, 128 * h:128 * (h + 1)] = y.astype(o_ref.dtype)


def _ret_tables():
    T = T_RET
    log_g = np.log1p(-np.exp2(-5.0 - np.arange(RET_HEADS, dtype=np.float64)))
    n = np.arange(T, dtype=np.float64)
    kscale = RET_KEY_DIM ** -0.5
    allowed = (np.arange(T)[None, :] // CHUNK) <= (np.arange(T)[:, None] // CHUNK)
    dmat = np.where(allowed[None],
                    np.exp(log_g[:, None, None] * np.abs(n[:, None] - n[None, :])[None]), 0.0) * kscale
    qdec = np.exp(log_g[:, None] * (n + 1.0)[None]) * kscale
    kdec = np.exp(log_g[:, None] * (T - 1.0 - n)[None])
    cdec = np.exp(log_g * T)
    qdec = np.broadcast_to(qdec[:, :, None], (RET_HEADS, T, 128))
    kdec = np.broadcast_to(kdec[:, :, None], (RET_HEADS, T, 128))
    cdec = np.broadcast_to(cdec[:, None, None], (RET_HEADS, 1, 128))
    return tuple(jnp.asarray(a, F32) for a in (dmat, qdec, kdec, cdec))


def _retention(proj, batch, seq, gn_g):
    T = T_RET
    nt = seq // T
    dmat, qdec, kdec, cdec = _ret_tables()
    const = lambda shape: pl.BlockSpec(shape, lambda b, t: (0,) * len(shape))
    return pl.pallas_call(
        _ret_kernel,
        grid=(batch, nt),
        in_specs=[
            pl.BlockSpec((T, 256), lambda b, t: (b * nt + t, COLB_RQ)),
            pl.BlockSpec((T, 256), lambda b, t: (b * nt + t, COLB_RK)),
            pl.BlockSpec((T, 512), lambda b, t: (b * nt + t, COLB_RV)),
            pl.BlockSpec((T, 512), lambda b, t: (b * nt + t, COLB_RG)),
            const((RET_HEADS, T, T)), const((RET_HEADS, T, 128)),
            const((RET_HEADS, T, 128)), const((RET_HEADS, 1, 128)),
            const((1, RET_WIDTH)),
        ],
        out_specs=pl.BlockSpec((T, RET_WIDTH), lambda b, t: (b * nt + t, 0)),
        out_shape=jax.ShapeDtypeStruct((batch * seq, RET_WIDTH), BF16),
        scratch_shapes=[pltpu.VMEM((RET_HEADS, 128, 128), F32)],
        compiler_params=pltpu.CompilerParams(
            dimension_semantics=("arbitrary", "arbitrary"), vmem_limit_bytes=VMEM_LIMIT),
        name="retention",
    )(proj, proj, proj, proj, dmat, qdec, kdec, cdec, gn_g)


def _ffn_kernel(tiles_per_seq, x_ref, od_ref, or_ref, wo_ref, g2_ref, wa_ref, wb_ref,
                cw_ref, cb_ref, wd_ref, gf_ref, o_ref, xn_scr, h_scr, halo_scr):
    TM = TM_FFN

    @pl.when(pl.program_id(0) % tiles_per_seq == 0)
    def _():
        halo_scr[...] = jnp.zeros(halo_scr.shape, F32)

    x1 = (x_ref[...]
          + jnp.dot(od_ref[...], wo_ref[0:DIFF_WIDTH, :], preferred_element_type=F32)
          + jnp.dot(or_ref[...], wo_ref[DIFF_WIDTH:, :], preferred_element_type=F32))
    o_ref[...] = x1
    xn_scr[...] = _rms(x1, g2_ref[...]).astype(BF16)

    row = lax.broadcasted_iota(jnp.int32, (TM, TF), 0)

    for c in range(NF):
        xn = xn_scr[...]
        a = jnp.dot(xn, wa_ref[c], preferred_element_type=F32)
        b = jnp.dot(xn, wb_ref[c], preferred_element_type=F32)
        halo = halo_scr[c]
        halo_scr[c] = a[TM - 8:TM, :]
        a1 = jnp.where(row == 0, halo[7:8, :], pltpu.roll(a, 1, 0))
        a2 = pltpu.roll(a, 2, 0)
        a2 = jnp.where(row == 0, halo[6:7, :], jnp.where(row == 1, halo[7:8, :], a2))
        cw = cw_ref[c]
        u = cw[0:1, :] * a2 + cw[1:2, :] * a1 + cw[2:3, :] * a + cb_ref[c]
        gelu = 0.5 * u * (1.0 + jnp.tanh(0.7978845608028654 * (u + 0.044715 * (u * u * u))))
        h_scr[:, c * TF:(c + 1) * TF] = (gelu * b).astype(BF16)

    y = o_ref[...] + jnp.dot(h_scr[...], wd_ref[...], preferred_element_type=F32)
    o_ref[...] = _rms(y, gf_ref[...])


def _ffn(x2, o_diff, o_ret, wo, g2, wa, wb, cw, cb, wd, gf, seq):
    n = x2.shape[0]
    TM = TM_FFN
    const = lambda shape: pl.BlockSpec(shape, lambda i: (0,) * len(shape),
                                       pipeline_mode=pl.Buffered(1))
    return pl.pallas_call(
        functools.partial(_ffn_kernel, seq // TM),
        grid=(n // TM,),
        in_specs=[
            pl.BlockSpec((TM, D_MODEL), lambda i: (i, 0)),
            pl.BlockSpec((TM, DIFF_WIDTH), lambda i: (i, 0)),
            pl.BlockSpec((TM, RET_WIDTH), lambda i: (i, 0)),
            const((D_MODEL, D_MODEL)), const((1, D_MODEL)),
            const((NF, D_MODEL, TF)), const((NF, D_MODEL, TF)),
            const((NF, 3, TF)), const((NF, 1, TF)),
            const((D_FF, D_MODEL)), const((1, D_MODEL)),
        ],
        out_specs=pl.BlockSpec((TM, D_MODEL), lambda i: (i, 0)),
        out_shape=jax.ShapeDtypeStruct((n, D_MODEL), F32),
        scratch_shapes=[
            pltpu.VMEM((TM, D_MODEL), BF16),
            pltpu.VMEM((TM, D_FF), BF16),
            pltpu.VMEM((NF, 8, TF), F32),
        ],
        compiler_params=pltpu.CompilerParams(
            dimension_semantics=("arbitrary",), vmem_limit_bytes=VMEM_LIMIT),
        name="outproj_ffn",
    )(x2, o_diff, o_ret, wo, g2, wa, wb, cw, cb, wd, gf)


def kernel(x, norm_mix_g, w_in, lambda_q1, lambda_k1, lambda_q2, lambda_k2, diff_subln_g,
           ret_gn_g, w_out, norm_ffn_g, w_up, conv_w, conv_b, w_down, final_norm_g):
    batch, seq, _ = x.shape
    x2 = x.reshape(batch * seq, D_MODEL)

    proj = _inproj(x2, norm_mix_g, w_in[0].astype(BF16))
    o_diff = _diff_attention(proj, batch, seq, lambda_q1, lambda_k1, lambda_q2, lambda_k2,
                             diff_subln_g)
    o_ret = _retention(proj, batch, seq, ret_gn_g)

    wa = w_up[0][:, :D_FF].astype(BF16).reshape(D_MODEL, NF, TF).transpose(1, 0, 2)
    wb = w_up[0][:, D_FF:].astype(BF16).reshape(D_MODEL, NF, TF).transpose(1, 0, 2)
    cw = conv_w[0].reshape(3, NF, TF).transpose(1, 0, 2)
    cb = conv_b[0].reshape(NF, 1, TF)
    wd = w_down[0].astype(BF16)
    y = _ffn(x2, o_diff, o_ret, w_out[0].astype(BF16), norm_ffn_g, wa, wb, cw, cb, wd,
             final_norm_g.reshape(1, D_MODEL), seq)
    return y.reshape(batch, seq, D_MODEL)
```

```python
import functools
import math

import jax
import jax.numpy as jnp
import numpy as np
from jax import lax
from jax.experimental import pallas as pl
from jax.experimental.pallas import tpu as pltpu

D_MODEL = 1024
CHUNK = 64
DIFF_HEADS = 4
DIFF_HEAD_DIM = 64
DIFF_V_DIM = 128
DIFF_WIDTH = 512
RET_HEADS = 4
RET_KEY_DIM = 64
RET_V_DIM = 128
RET_WIDTH = 512
D_FF = 2816
IN_WIDTH = 3072
EPS = 1e-6
LAM_INIT = 0.8 - 0.6 * math.exp(-0.3 * 0)
LOG2E = math.log2(math.e)
MASKED = -1e30

COL_DQ, COL_DK, COL_DV = 0, 4, 8
COLB_RQ, COLB_RK = 6, 7
COLB_RV, COLB_RG = 4, 5

TM_IN = 512
TQ_ATT = 512
TK_ATT = 512
V_ROWS = DIFF_V_DIM + 16
EXP2_UNDERFLOW = 150.0
BOUND_MARGIN = 2.0
NORM_SLACK = 1.001
KNORM_SLACK = 1.002
T_RET = 256
TM_FFN = 512
TF = 256
NF = D_FF // TF
VMEM_LIMIT = 56 * 1024 * 1024

BF16 = jnp.bfloat16
F32 = jnp.float32


def _rms(x, g):
    ms = jnp.mean(x * x, axis=-1, keepdims=True)
    return x * lax.rsqrt(ms + EPS) * g


def _inproj_kernel(x_ref, g_ref, w_ref, o_ref):
    h = _rms(x_ref[...], g_ref[...]).astype(BF16)
    o_ref[...] = jnp.dot(h, w_ref[...], preferred_element_type=F32).astype(o_ref.dtype)


def _inproj(x2, g, w_bf16):
    n = x2.shape[0]
    return pl.pallas_call(
        _inproj_kernel,
        grid=(n // TM_IN,),
        in_specs=[
            pl.BlockSpec((TM_IN, D_MODEL), lambda i: (i, 0)),
            pl.BlockSpec((1, D_MODEL), lambda i: (0, 0)),
            pl.BlockSpec((D_MODEL, IN_WIDTH), lambda i: (0, 0), pipeline_mode=pl.Buffered(1)),
        ],
        out_specs=pl.BlockSpec((TM_IN, IN_WIDTH), lambda i: (i, 0)),
        out_shape=jax.ShapeDtypeStruct((n, IN_WIDTH), BF16),
        compiler_params=pltpu.CompilerParams(
            dimension_semantics=("arbitrary",), vmem_limit_bytes=VMEM_LIMIT),
        name="inproj",
    )(x2, g, w_bf16)


def _attn_kernel(q_ref, k_ref, v_ref, postab_ref, ctab_ref, cpos_ref, corr_ref,
                 lq1_ref, lk1_ref, lq2_ref, lk2_ref, g_ref, o_ref,
                 kaug_scr, vt_scr, qt_scr, first_scr, s_scr, m_scr, acc_scr):
    TQ, TK = TQ_ATT, TK_ATT
    DV, DH = DIFF_V_DIM, DIFF_HEAD_DIM
    seq = k_ref.shape[0]
    n_tiles = seq // TK
    tile_lane = lax.broadcasted_iota(jnp.int32, (1, 128), 1)
    lane = lax.broadcasted_iota(jnp.int32, (TK, 128), 1)
    ctab = ctab_ref[0]
    slope_end = cpos_ref[0, 0:1, :]
    slope = cpos_ref[0, 1:2, :]

    ones_row = lax.broadcasted_iota(jnp.int32, (128, 128), 0)
    ones_col = lax.broadcasted_iota(jnp.int32, (128, 128), 1)
    half_ones = jnp.where((ones_row < DH) == (ones_col < DH), 1.0, 0.0).astype(BF16)
    prefix_max = [jnp.zeros((1, 128), F32)] * 2
    running = [jnp.zeros((1, 1), F32)] * 2
    firsts = jnp.zeros((1, 128), jnp.int32)
    for c in range(n_tiles):
        rows = slice(c * TK, (c + 1) * TK)
        kc = k_ref[rows, :]
        pt = postab_ref[rows, :]
        kaug_scr[0, rows, :] = jnp.where(lane < DH, kc, pt)
        kaug_scr[1, rows, :] = jnp.where(lane >= DH, kc, pt)
        vt_scr[c, 0:DV, :] = v_ref[rows, :].T
        vt_scr[c, DV:, :] = jnp.ones((V_ROWS - DV, TK), BF16)
        ksq = kc.astype(F32)
        ksq = (ksq * ksq).astype(BF16)
        n2 = jnp.max(jnp.dot(ksq, half_ones, preferred_element_type=F32),
                     axis=0, keepdims=True)
        q = q_ref[rows, :].astype(F32) * (DH ** -0.5 * LOG2E)
        skip = tile_lane < c
        for half in range(2):
            in_half = (lane >= DH) if half else (lane < DH)
            in_half_row = (tile_lane >= DH) if half else (tile_lane < DH)
            k_norm = jnp.sqrt(jnp.max(jnp.where(in_half_row, n2, 0.0), axis=1,
                                      keepdims=True)) * KNORM_SLACK
            qb = jnp.where(in_half, q, ctab).astype(BF16).T
            qt_scr[c, half] = qb
            qf = qb[half * DH:(half + 1) * DH, :].astype(F32)
            q_norm = jnp.sqrt(jnp.max(jnp.sum(qf * qf, axis=0, keepdims=True),
                                      axis=1, keepdims=True)) * NORM_SLACK
            upper = q_norm * prefix_max[half] + slope_end
            lower = slope * float(c * TQ) - q_norm * k_norm
            skip = jnp.logical_and(skip, upper + BOUND_MARGIN < lower - EXP2_UNDERFLOW)
            running[half] = jnp.maximum(running[half], k_norm)
            prefix_max[half] = jnp.where(tile_lane == c, running[half], prefix_max[half])
        first_c = jnp.sum(skip.astype(jnp.int32), axis=1, keepdims=True)
        firsts = jnp.where(tile_lane == c, first_c, firsts)
    first_scr[...] = firsts
    acc_scr[...] = jnp.ones(acc_scr.shape, F32)

    lam = (jnp.exp(jnp.sum(lq1_ref[...] * lk1_ref[...], axis=1, keepdims=True))
           - jnp.exp(jnp.sum(lq2_ref[...] * lk2_ref[...], axis=1, keepdims=True))
           + LAM_INIT)

    def scores(i, j, slot, halves=(0, 1)):
        start = pl.multiple_of(j * TK, TK)
        for half in halves:
            s_scr[slot, half] = jnp.dot(kaug_scr[half, pl.ds(start, TK), :], qt_scr[i, half],
                                        preferred_element_type=F32)

    def consume(j, slot, corr, halves=(0, 1)):
        vt = vt_scr[j]
        for half in halves:
            s = s_scr[slot, half]
            if corr is not None:
                s = s + corr
            m_prev = m_scr[half]
            m_new = jnp.maximum(m_prev, jnp.max(s, axis=0, keepdims=True))
            alpha = jnp.exp2(m_prev - m_new)
            p = jnp.exp2(s - m_new).astype(BF16)
            acc_scr[half] = alpha * acc_scr[half] + jnp.dot(
                vt, p, preferred_element_type=F32)
            m_scr[half] = m_new

    def finish(i):
        inv_l0 = 1.0 / acc_scr[0, DV:DV + 1, :]
        inv_l1 = 1.0 / acc_scr[1, DV:DV + 1, :]
        o_t = acc_scr[0, 0:DV, :] * inv_l0 - lam * (acc_scr[1, 0:DV, :] * inv_l1)
        ms = jnp.mean(o_t * o_t, axis=0, keepdims=True)
        y = (o_t * lax.rsqrt(ms + EPS)).T
        row0 = pl.multiple_of(i * TQ, TQ)
        o_ref[pl.ds(row0, TQ), :] = (y * g_ref[...] * (1.0 - LAM_INIT)).astype(o_ref.dtype)

    def q_tile(i, carry):
        first = jnp.sum(jnp.where(tile_lane == i, first_scr[...], 0))
        n_past = i - first
        finish(jnp.maximum(i - 1, 0))
        scores(i, first, 0)
        m_scr[...] = jnp.full(m_scr.shape, MASKED, F32)
        acc_scr[...] = jnp.zeros(acc_scr.shape, F32)

        def tile_pair(t, c):
            j = first + 2 * t
            scores(i, j + 1, 1)
            for half in range(2):
                consume(j, 0, None, (half,))
                scores(i, j + 2, 0, (half,))
            consume(j + 1, 1, None)
            return c

        lax.fori_loop(0, n_past // 2, tile_pair, 0)

        @pl.when(n_past % 2 == 0)
        def _():
            consume(i, 0, corr_ref[0])

        @pl.when(n_past % 2 == 1)
        def _():
            scores(i, i, 1)
            consume(i - 1, 0, None)
            consume(i, 1, corr_ref[0])

        return carry

    lax.fori_loop(0, seq // TQ, q_tile, 0)
    finish(seq // TQ - 1)


def _bf16_pieces(x, n):
    x = np.asarray(x, np.float32)
    pieces = []
    for _ in range(n):
        p = x.astype(BF16).astype(np.float32)
        pieces.append(p)
        x = x - p
    return pieces


def _attn_tables(seq):
    TQ, TK = TQ_ATT, TK_ATT
    slopes = (np.exp2(-8.0 * np.arange(1, DIFF_HEADS + 1, dtype=np.float64) / DIFF_HEADS)
              * LOG2E).astype(np.float32)
    c1, c2, c3 = _bf16_pieces(slopes, 3)
    ctab = np.zeros((DIFF_HEADS, 1, 128), np.float32)
    pos = np.arange(seq)
    p_hi, p_lo = (pos // 64) * 64, pos % 64
    postab = np.zeros((seq, 128), np.float32)
    for base in (0, DIFF_HEAD_DIM):
        for n, (cp, pp) in enumerate(((c1, p_hi), (c2, p_hi), (c3, p_hi),
                                      (c1, p_lo), (c2, p_lo), (c3, p_lo))):
            ctab[:, 0, base + n] = cp
            postab[:, base + n] = pp
    c = np.arange(TK)[:, None]
    r = np.arange(TQ)[None, :]
    allowed = (c // CHUNK) <= (r // CHUNK)
    ahead = np.where(c > r, 2.0 * (r - c), 0.0)
    corr = np.where(allowed[None], slopes.astype(np.float64)[:, None, None] * ahead[None], MASKED)
    cpos = np.zeros((DIFF_HEADS, 2, 128), np.float64)
    cpos[:, 0, :] = slopes.astype(np.float64)[:, None] * ((np.arange(128) + 1) * TK - 1)[None, :]
    cpos[:, 1, :] = slopes.astype(np.float64)[:, None]
    return (jnp.asarray(postab, BF16), jnp.asarray(ctab, F32), jnp.asarray(cpos, F32),
            jnp.asarray(corr, F32))


def _diff_attention(proj, batch, seq, lq1, lk1, lq2, lk2, subln_g):
    TQ, TK = TQ_ATT, TK_ATT
    assert TQ == TK and seq % TK == 0
    assert seq // TK <= 128
    postab, ctab, cpos, corr = _attn_tables(seq)
    small = lambda shape: pl.BlockSpec(shape, lambda b, h: (0,) * len(shape))
    return pl.pallas_call(
        _attn_kernel,
        grid=(batch, DIFF_HEADS),
        in_specs=[
            pl.BlockSpec((seq, 128), lambda b, h: (b, COL_DQ + h)),
            pl.BlockSpec((seq, 128), lambda b, h: (b, COL_DK + h)),
            pl.BlockSpec((seq, 128), lambda b, h: (b, COL_DV + h)),
            pl.BlockSpec((seq, 128), lambda b, h: (0, 0), pipeline_mode=pl.Buffered(1)),
            pl.BlockSpec((1, 1, 128), lambda b, h: (h, 0, 0)),
            pl.BlockSpec((1, 2, 128), lambda b, h: (h, 0, 0)),
            pl.BlockSpec((1, TK, TQ), lambda b, h: (h, 0, 0)),
            small((1, DIFF_HEAD_DIM)), small((1, DIFF_HEAD_DIM)),
            small((1, DIFF_HEAD_DIM)), small((1, DIFF_HEAD_DIM)),
            small((1, DIFF_V_DIM)),
        ],
        out_specs=pl.BlockSpec((seq, DIFF_V_DIM), lambda b, h: (b, h)),
        out_shape=jax.ShapeDtypeStruct((batch * seq, DIFF_WIDTH), BF16),
        scratch_shapes=[
            pltpu.VMEM((2, seq, 128), BF16),
            pltpu.VMEM((seq // TK, V_ROWS, TK), BF16),
            pltpu.VMEM((seq // TQ, 2, 128, TQ), BF16),
            pltpu.VMEM((1, 128), jnp.int32),
            pltpu.VMEM((2, 2, TK, TQ), F32),
            pltpu.VMEM((2, 1, TQ), F32),
            pltpu.VMEM((2, V_ROWS, TQ), F32),
        ],
        compiler_params=pltpu.CompilerParams(
            dimension_semantics=("arbitrary", "arbitrary"), vmem_limit_bytes=VMEM_LIMIT),
        name="diff_attn",
    )(proj, proj, proj, postab, ctab, cpos, corr, lq1, lk1, lq2, lk2, subln_g)


def _ret_kernel(q_ref, k_ref, v_ref, gate_ref, dmat_ref, qdec_ref, kdec_ref, cdec_ref,
                gn_ref, o_ref, state_scr):
    T = T_RET

    @pl.when(pl.program_id(1) == 0)
    def _():
        state_scr[...] = jnp.zeros(state_scr.shape, F32)

    lane = lax.broadcasted_iota(jnp.int32, (T, 128), 1)
    for h in range(RET_HEADS):
        pair, half = divmod(h, 2)
        qp = q_ref[:, 128 * pair:128 * (pair + 1)]
        kp = k_ref[:, 128 * pair:128 * (pair + 1)]
        in_head = (lane >= RET_KEY_DIM) if half else (lane < RET_KEY_DIM)
        km = jnp.where(in_head, kp, jnp.zeros_like(kp))
        v = v_ref[:, 128 * h:128 * (h + 1)]
        s = lax.dot_general(qp, km, (((1,), (1,)), ((), ())), preferred_element_type=F32)
        s = s * dmat_ref[h]
        o = jnp.dot(s.astype(BF16), v, preferred_element_type=F32)
        state = state_scr[h]
        o = o + qdec_ref[h] * jnp.dot(qp, state.astype(BF16), preferred_element_type=F32)
        vd = (v.astype(F32) * kdec_ref[h]).astype(BF16)
        state_scr[h] = cdec_ref[h] * state + lax.dot_general(
            km, vd, (((0,), (0,)), ((), ())), preferred_element_type=F32)

        mu = jnp.mean(o, axis=-1, keepdims=True)
        d = o - mu
        var = jnp.mean(d * d, axis=-1, keepdims=True)
        y = d * lax.rsqrt(var + EPS) * gn_ref[:, 128 * h:128 * (h + 1)]
        gate = gate_ref[:, 128 * h:128 * (h + 1)].astype(F32)
        y = y * (gate * (1.0 / (1.0 + jnp.exp(-gate))))
        o_ref[:, 128 * h:128 * (h + 1)] = y.astype(o_ref.dtype)


def _ret_tables():
    T = T_RET
    log_g = np.log1p(-np.exp2(-5.0 - np.arange(RET_HEADS, dtype=np.float64)))
    n = np.arange(T, dtype=np.float64)
    kscale = RET_KEY_DIM ** -0.5
    allowed = (np.arange(T)[None, :] // CHUNK) <= (np.arange(T)[:, None] // CHUNK)
    dmat = np.where(allowed[None],
                    np.exp(log_g[:, None, None] * np.abs(n[:, None] - n[None, :])[None]), 0.0) * kscale
    qdec = np.exp(log_g[:, None] * (n + 1.0)[None]) * kscale
    kdec = np.exp(log_g[:, None] * (T - 1.0 - n)[None])
    cdec = np.exp(log_g * T)
    qdec = np.broadcast_to(qdec[:, :, None], (RET_HEADS, T, 128))
    kdec = np.broadcast_to(kdec[:, :, None], (RET_HEADS, T, 128))
    cdec = np.broadcast_to(cdec[:, None, None], (RET_HEADS, 1, 128))
    return tuple(jnp.asarray(a, F32) for a in (dmat, qdec, kdec, cdec))


def _retention(proj, batch, seq, gn_g):
    T = T_RET
    nt = seq // T
    dmat, qdec, kdec, cdec = _ret_tables()
    const = lambda shape: pl.BlockSpec(shape, lambda b, t: (0,) * len(shape))
    return pl.pallas_call(
        _ret_kernel,
        grid=(batch, nt),
        in_specs=[
            pl.BlockSpec((T, 256), lambda b, t: (b * nt + t, COLB_RQ)),
            pl.BlockSpec((T, 256), lambda b, t: (b * nt + t, COLB_RK)),
            pl.BlockSpec((T, 512), lambda b, t: (b * nt + t, COLB_RV)),
            pl.BlockSpec((T, 512), lambda b, t: (b * nt + t, COLB_RG)),
            const((RET_HEADS, T, T)), const((RET_HEADS, T, 128)),
            const((RET_HEADS, T, 128)), const((RET_HEADS, 1, 128)),
            const((1, RET_WIDTH)),
        ],
        out_specs=pl.BlockSpec((T, RET_WIDTH), lambda b, t: (b * nt + t, 0)),
        out_shape=jax.ShapeDtypeStruct((batch * seq, RET_WIDTH), BF16),
        scratch_shapes=[pltpu.VMEM((RET_HEADS, 128, 128), F32)],
        compiler_params=pltpu.CompilerParams(
            dimension_semantics=("arbitrary", "arbitrary"), vmem_limit_bytes=VMEM_LIMIT),
        name="retention",
    )(proj, proj, proj, proj, dmat, qdec, kdec, cdec, gn_g)


def _ffn_kernel(tiles_per_seq, x_ref, od_ref, or_ref, wo_ref, g2_ref, wa_ref, wb_ref,
                cw_ref, cb_ref, wd_ref, gf_ref, o_ref, xn_scr, h_scr, halo_scr):
    TM = TM_FFN

    @pl.when(pl.program_id(0) % tiles_per_seq == 0)
    def _():
        halo_scr[...] = jnp.zeros(halo_scr.shape, F32)

    x1 = (x_ref[...]
          + jnp.dot(od_ref[...], wo_ref[0:DIFF_WIDTH, :], preferred_element_type=F32)
          + jnp.dot(or_ref[...], wo_ref[DIFF_WIDTH:, :], preferred_element_type=F32))
    o_ref[...] = x1
    xn_scr[...] = _rms(x1, g2_ref[...]).astype(BF16)

    row = lax.broadcasted_iota(jnp.int32, (TM, TF), 0)

    for c in range(NF):
        xn = xn_scr[...]
        a = jnp.dot(xn, wa_ref[c], preferred_element_type=F32)
        b = jnp.dot(xn, wb_ref[c], preferred_element_type=F32)
        halo = halo_scr[c]
        halo_scr[c] = a[TM - 8:TM, :]
        a1 = jnp.where(row == 0, halo[7:8, :], pltpu.roll(a, 1, 0))
        a2 = pltpu.roll(a, 2, 0)
        a2 = jnp.where(row == 0, halo[6:7, :], jnp.where(row == 1, halo[7:8, :], a2))
        cw = cw_ref[c]
        u = cw[0:1, :] * a2 + cw[1:2, :] * a1 + cw[2:3, :] * a + cb_ref[c]
        gelu = 0.5 * u * (1.0 + jnp.tanh(0.7978845608028654 * (u + 0.044715 * (u * u * u))))
        h_scr[:, c * TF:(c + 1) * TF] = (gelu * b).astype(BF16)

    y = o_ref[...] + jnp.dot(h_scr[...], wd_ref[...], preferred_element_type=F32)
    o_ref[...] = _rms(y, gf_ref[...])


def _ffn(x2, o_diff, o_ret, wo, g2, wa, wb, cw, cb, wd, gf, seq):
    n = x2.shape[0]
    TM = TM_FFN
    const = lambda shape: pl.BlockSpec(shape, lambda i: (0,) * len(shape),
                                       pipeline_mode=pl.Buffered(1))
    return pl.pallas_call(
        functools.partial(_ffn_kernel, seq // TM),
        grid=(n // TM,),
        in_specs=[
            pl.BlockSpec((TM, D_MODEL), lambda i: (i, 0)),
            pl.BlockSpec((TM, DIFF_WIDTH), lambda i: (i, 0)),
            pl.BlockSpec((TM, RET_WIDTH), lambda i: (i, 0)),
            const((D_MODEL, D_MODEL)), const((1, D_MODEL)),
            const((NF, D_MODEL, TF)), const((NF, D_MODEL, TF)),
            const((NF, 3, TF)), const((NF, 1, TF)),
            const((D_FF, D_MODEL)), const((1, D_MODEL)),
        ],
        out_specs=pl.BlockSpec((TM, D_MODEL), lambda i: (i, 0)),
        out_shape=jax.ShapeDtypeStruct((n, D_MODEL), F32),
        scratch_shapes=[
            pltpu.VMEM((TM, D_MODEL), BF16),
            pltpu.VMEM((TM, D_FF), BF16),
            pltpu.VMEM((NF, 8, TF), F32),
        ],
        compiler_params=pltpu.CompilerParams(
            dimension_semantics=("arbitrary",), vmem_limit_bytes=VMEM_LIMIT),
        name="outproj_ffn",
    )(x2, o_diff, o_ret, wo, g2, wa, wb, cw, cb, wd, gf)


def kernel(x, norm_mix_g, w_in, lambda_q1, lambda_k1, lambda_q2, lambda_k2, diff_subln_g,
           ret_gn_g, w_out, norm_ffn_g, w_up, conv_w, conv_b, w_down, final_norm_g):
    batch, seq, _ = x.shape
    x2 = x.reshape(batch * seq, D_MODEL)

    proj = _inproj(x2, norm_mix_g, w_in[0].astype(BF16))
    o_diff = _diff_attention(proj, batch, seq, lambda_q1, lambda_k1, lambda_q2, lambda_k2,
                             diff_subln_g)
    o_ret = _retention(proj, batch, seq, ret_gn_g)

    wa = w_up[0][:, :D_FF].astype(BF16).reshape(D_MODEL, NF, TF).transpose(1, 0, 2)
    wb = w_up[0][:, D_FF:].astype(BF16).reshape(D_MODEL, NF, TF).transpose(1, 0, 2)
    cw = conv_w[0].reshape(3, NF, TF).transpose(1, 0, 2)
    cb = conv_b[0].reshape(NF, 1, TF)
    wd = w_down[0].astype(BF16)
    y = _ffn(x2, o_diff, o_ret, w_out[0].astype(BF16), norm_ffn_g, wa, wb, cw, cb, wd,
             final_norm_g.reshape(1, D_MODEL), seq)
    return y.reshape(batch, seq, D_MODEL)
```

```python
import functools
import math

import jax
import jax.numpy as jnp
import numpy as np
from jax import lax
from jax.experimental import pallas as pl
from jax.experimental.pallas import tpu as pltpu

D_MODEL = 1024
CHUNK = 64
DIFF_HEADS = 4
DIFF_HEAD_DIM = 64
DIFF_V_DIM = 128
DIFF_WIDTH = 512
RET_HEADS = 4
RET_KEY_DIM = 64
RET_V_DIM = 128
RET_WIDTH = 512
D_FF = 2816
IN_WIDTH = 3072
EPS = 1e-6
LAM_INIT = 0.8 - 0.6 * math.exp(-0.3 * 0)
LOG2E = math.log2(math.e)
MASKED = -1e30

COL_DQ, COL_DK, COL_DV = 0, 4, 8
COLB_RQ, COLB_RK = 6, 7
COLB_RV, COLB_RG = 4, 5

TM_IN = 512
TQ_ATT = 512
TK_ATT = 512
V_ROWS = DIFF_V_DIM + 16
EXP2_UNDERFLOW = 150.0
BOUND_MARGIN = 2.0
NORM_SLACK = 1.001
KNORM_SLACK = 1.002
T_RET = 256
TM_FFN = 512
TF = 256
NF = D_FF // TF
VMEM_LIMIT = 56 * 1024 * 1024

BF16 = jnp.bfloat16
F32 = jnp.float32


def _rms(x, g):
    ms = jnp.mean(x * x, axis=-1, keepdims=True)
    return x * lax.rsqrt(ms + EPS) * g


def _inproj_kernel(x_ref, g_ref, w_ref, o_ref):
    h = _rms(x_ref[...], g_ref[...]).astype(BF16)
    o_ref[...] = jnp.dot(h, w_ref[...], preferred_element_type=F32).astype(o_ref.dtype)


def _inproj(x2, g, w_bf16):
    n = x2.shape[0]
    return pl.pallas_call(
        _inproj_kernel,
        grid=(n // TM_IN,),
        in_specs=[
            pl.BlockSpec((TM_IN, D_MODEL), lambda i: (i, 0)),
            pl.BlockSpec((1, D_MODEL), lambda i: (0, 0)),
            pl.BlockSpec((D_MODEL, IN_WIDTH), lambda i: (0, 0), pipeline_mode=pl.Buffered(1)),
        ],
        out_specs=pl.BlockSpec((TM_IN, IN_WIDTH), lambda i: (i, 0)),
        out_shape=jax.ShapeDtypeStruct((n, IN_WIDTH), BF16),
        compiler_params=pltpu.CompilerParams(
            dimension_semantics=("arbitrary",), vmem_limit_bytes=VMEM_LIMIT),
        name="inproj",
    )(x2, g, w_bf16)


def _attn_kernel(q_ref, k_ref, v_ref, postab_ref, ctab_ref, cpos_ref, corr_ref,
                 lq1_ref, lk1_ref, lq2_ref, lk2_ref, g_ref, o_ref,
                 kaug_scr, vt_scr, qt_scr, first_scr, s_scr, m_scr, acc_scr):
    TQ, TK = TQ_ATT, TK_ATT
    DV, DH = DIFF_V_DIM, DIFF_HEAD_DIM
    seq = k_ref.shape[0]
    n_tiles = seq // TK
    tile_lane = lax.broadcasted_iota(jnp.int32, (1, 128), 1)
    lane = lax.broadcasted_iota(jnp.int32, (TK, 128), 1)
    ctab = ctab_ref[0]
    slope_end = cpos_ref[0, 0:1, :]
    slope = cpos_ref[0, 1:2, :]

    ones_row = lax.broadcasted_iota(jnp.int32, (128, 128), 0)
    ones_col = lax.broadcasted_iota(jnp.int32, (128, 128), 1)
    half_ones = jnp.where((ones_row < DH) == (ones_col < DH), 1.0, 0.0).astype(BF16)
    prefix_max = [jnp.zeros((1, 128), F32)] * 2
    running = [jnp.zeros((1, 1), F32)] * 2
    firsts = jnp.zeros((1, 128), jnp.int32)
    for c in range(n_tiles):
        rows = slice(c * TK, (c + 1) * TK)
        kc = k_ref[rows, :]
        pt = postab_ref[rows, :]
        kaug_scr[0, rows, :] = jnp.where(lane < DH, kc, pt)
        kaug_scr[1, rows, :] = jnp.where(lane >= DH, kc, pt)
        vt_scr[c, 0:DV, :] = v_ref[rows, :].T
        vt_scr[c, DV:, :] = jnp.ones((V_ROWS - DV, TK), BF16)
        ksq = kc.astype(F32)
        ksq = (ksq * ksq).astype(BF16)
        n2 = jnp.max(jnp.dot(ksq, half_ones, preferred_element_type=F32),
                     axis=0, keepdims=True)
        q = q_ref[rows, :].astype(F32) * (DH ** -0.5 * LOG2E)
        skip = tile_lane < c
        for half in range(2):
            in_half = (lane >= DH) if half else (lane < DH)
            in_half_row = (tile_lane >= DH) if half else (tile_lane < DH)
            k_norm = jnp.sqrt(jnp.max(jnp.where(in_half_row, n2, 0.0), axis=1,
                                      keepdims=True)) * KNORM_SLACK
            qb = jnp.where(in_half, q, ctab).astype(BF16).T
            qt_scr[c, half] = qb
            qf = qb[half * DH:(half + 1) * DH, :].astype(F32)
            q_norm = jnp.sqrt(jnp.max(jnp.sum(qf * qf, axis=0, keepdims=True),
                                      axis=1, keepdims=True)) * NORM_SLACK
            upper = q_norm * prefix_max[half] + slope_end
            lower = slope * float(c * TQ) - q_norm * k_norm
            skip = jnp.logical_and(skip, upper + BOUND_MARGIN < lower - EXP2_UNDERFLOW)
            running[half] = jnp.maximum(running[half], k_norm)
            prefix_max[half] = jnp.where(tile_lane == c, running[half], prefix_max[half])
        first_c = jnp.sum(skip.astype(jnp.int32), axis=1, keepdims=True)
        firsts = jnp.where(tile_lane == c, first_c, firsts)
    first_scr[...] = firsts
    acc_scr[...] = jnp.ones(acc_scr.shape, F32)

    lam = (jnp.exp(jnp.sum(lq1_ref[...] * lk1_ref[...], axis=1, keepdims=True))
           - jnp.exp(jnp.sum(lq2_ref[...] * lk2_ref[...], axis=1, keepdims=True))
           + LAM_INIT)

    def scores(i, j, slot, halves=(0, 1)):
        start = pl.multiple_of(j * TK, TK)
        for half in halves:
            s_scr[slot, half] = jnp.dot(kaug_scr[half, pl.ds(start, TK), :], qt_scr[i, half],
                                        preferred_element_type=F32)

    def consume(j, slot, corr, halves=(0, 1)):
        vt = vt_scr[j]
        for half in halves:
            s = s_scr[slot, half]
            if corr is not None:
                s = s + corr
            m_prev = m_scr[half]
            m_new = jnp.maximum(m_prev, jnp.max(s, axis=0, keepdims=True))
            alpha = jnp.exp2(m_prev - m_new)
            p = jnp.exp2(s - m_new).astype(BF16)
            acc_scr[half] = alpha * acc_scr[half] + jnp.dot(
                vt, p, preferred_element_type=F32)
            m_scr[half] = m_new

    def finish(i):
        inv_l0 = 1.0 / acc_scr[0, DV:DV + 1, :]
        inv_l1 = 1.0 / acc_scr[1, DV:DV + 1, :]
        o_t = acc_scr[0, 0:DV, :] * inv_l0 - lam * (acc_scr[1, 0:DV, :] * inv_l1)
        ms = jnp.mean(o_t * o_t, axis=0, keepdims=True)
        y = (o_t * lax.rsqrt(ms + EPS)).T
        row0 = pl.multiple_of(i * TQ, TQ)
        o_ref[pl.ds(row0, TQ), :] = (y * g_ref[...] * (1.0 - LAM_INIT)).astype(o_ref.dtype)

    n_q = seq // TQ

    def q_tile(i, carry):
        firsts_v = first_scr[...]
        nxt = jnp.minimum(i + 1, n_q - 1)
        first = jnp.sum(jnp.where(tile_lane == i, firsts_v, 0))
        first_nxt = jnp.sum(jnp.where(tile_lane == nxt, firsts_v, 0))
        n_past = i - first
        n_rem = n_past - 1

        def start():
            finish(jnp.maximum(i - 1, 0))
            m_scr[...] = jnp.full(m_scr.shape, MASKED, F32)
            acc_scr[...] = jnp.zeros(acc_scr.shape, F32)

        @pl.when(n_past == 0)
        def _():
            start()
            for half in range(2):
                consume(i, 2, corr_ref[0], (half,))
                scores(nxt, first_nxt, 2, (half,))

        @pl.when(n_past > 0)
        def _():
            scores(i, first + 1, 0)
            start()
            consume(first, 2, None)

        def tile_group(j, n):
            scores(i, j + 1, 1)
            for u in range(n - 1):
                for half in range(2):
                    consume(j + u, u % 2, None, (half,))
                    scores(i, j + u + 2, u % 2, (half,))
            consume(j + n - 1, (n - 1) % 2, None)

        def group_loop(base, trips, n):
            def body(t, c):
                tile_group(base + n * t, n)
                return c
            lax.fori_loop(0, trips, body, 0)

        n_loop = jnp.maximum(n_rem, 0)
        group_loop(first + 1, n_loop // 4, 4)
        group_loop(first + 1 + 4 * (n_loop // 4), (n_loop % 4) // 2, 2)

        @pl.when(jnp.logical_and(n_past > 0, n_rem % 2 == 0))
        def _():
            scores(nxt, first_nxt, 2)
            consume(i, 0, corr_ref[0])

        @pl.when(jnp.logical_and(n_past > 0, n_rem % 2 == 1))
        def _():
            scores(i, i, 1)
            consume(i - 1, 0, None)
            scores(nxt, first_nxt, 2)
            consume(i, 1, corr_ref[0])

        return carry

    scores(0, 0, 2)
    lax.fori_loop(0, n_q, q_tile, 0)
    finish(n_q - 1)


def _bf16_pieces(x, n):
    x = np.asarray(x, np.float32)
    pieces = []
    for _ in range(n):
        p = x.astype(BF16).astype(np.float32)
        pieces.append(p)
        x = x - p
    return pieces


def _attn_tables(seq):
    TQ, TK = TQ_ATT, TK_ATT
    slopes = (np.exp2(-8.0 * np.arange(1, DIFF_HEADS + 1, dtype=np.float64) / DIFF_HEADS)
              * LOG2E).astype(np.float32)
    c1, c2, c3 = _bf16_pieces(slopes, 3)
    ctab = np.zeros((DIFF_HEADS, 1, 128), np.float32)
    pos = np.arange(seq)
    p_hi, p_lo = (pos // 64) * 64, pos % 64
    postab = np.zeros((seq, 128), np.float32)
    for base in (0, DIFF_HEAD_DIM):
        for n, (cp, pp) in enumerate(((c1, p_hi), (c2, p_hi), (c3, p_hi),
                                      (c1, p_lo), (c2, p_lo), (c3, p_lo))):
            ctab[:, 0, base + n] = cp
            postab[:, base + n] = pp
    c = np.arange(TK)[:, None]
    r = np.arange(TQ)[None, :]
    allowed = (c // CHUNK) <= (r // CHUNK)
    ahead = np.where(c > r, 2.0 * (r - c), 0.0)
    corr = np.where(allowed[None], slopes.astype(np.float64)[:, None, None] * ahead[None], MASKED)
    cpos = np.zeros((DIFF_HEADS, 2, 128), np.float64)
    cpos[:, 0, :] = slopes.astype(np.float64)[:, None] * ((np.arange(128) + 1) * TK - 1)[None, :]
    cpos[:, 1, :] = slopes.astype(np.float64)[:, None]
    return (jnp.asarray(postab, BF16), jnp.asarray(ctab, F32), jnp.asarray(cpos, F32),
            jnp.asarray(corr, F32))


def _diff_attention(proj, batch, seq, lq1, lk1, lq2, lk2, subln_g):
    TQ, TK = TQ_ATT, TK_ATT
    assert TQ == TK and seq % TK == 0
    assert seq // TK <= 128
    postab, ctab, cpos, corr = _attn_tables(seq)
    small = lambda shape: pl.BlockSpec(shape, lambda b, h: (0,) * len(shape))
    return pl.pallas_call(
        _attn_kernel,
        grid=(batch, DIFF_HEADS),
        in_specs=[
            pl.BlockSpec((seq, 128), lambda b, h: (b, COL_DQ + h)),
            pl.BlockSpec((seq, 128), lambda b, h: (b, COL_DK + h)),
            pl.BlockSpec((seq, 128), lambda b, h: (b, COL_DV + h)),
            pl.BlockSpec((seq, 128), lambda b, h: (0, 0), pipeline_mode=pl.Buffered(1)),
            pl.BlockSpec((1, 1, 128), lambda b, h: (h, 0, 0)),
            pl.BlockSpec((1, 2, 128), lambda b, h: (h, 0, 0)),
            pl.BlockSpec((1, TK, TQ), lambda b, h: (h, 0, 0)),
            small((1, DIFF_HEAD_DIM)), small((1, DIFF_HEAD_DIM)),
            small((1, DIFF_HEAD_DIM)), small((1, DIFF_HEAD_DIM)),
            small((1, DIFF_V_DIM)),
        ],
        out_specs=pl.BlockSpec((seq, DIFF_V_DIM), lambda b, h: (b, h)),
        out_shape=jax.ShapeDtypeStruct((batch * seq, DIFF_WIDTH), BF16),
        scratch_shapes=[
            pltpu.VMEM((2, seq, 128), BF16),
            pltpu.VMEM((seq // TK, V_ROWS, TK), BF16),
            pltpu.VMEM((seq // TQ, 2, 128, TQ), BF16),
            pltpu.VMEM((1, 128), jnp.int32),
            pltpu.VMEM((3, 2, TK, TQ), F32),
            pltpu.VMEM((2, 1, TQ), F32),
            pltpu.VMEM((2, V_ROWS, TQ), F32),
        ],
        compiler_params=pltpu.CompilerParams(
            dimension_semantics=("arbitrary", "arbitrary"), vmem_limit_bytes=VMEM_LIMIT),
        name="diff_attn",
    )(proj, proj, proj, postab, ctab, cpos, corr, lq1, lk1, lq2, lk2, subln_g)


def _ret_kernel(q_ref, k_ref, v_ref, gate_ref, dmat_ref, qdec_ref, kdec_ref, cdec_ref,
                gn_ref, o_ref, state_scr):
    T = T_RET

    @pl.when(pl.program_id(1) == 0)
    def _():
        state_scr[...] = jnp.zeros(state_scr.shape, F32)

    lane = lax.broadcasted_iota(jnp.int32, (T, 128), 1)
    heads = range(RET_HEADS)
    qp, km, v, s, outs = {}, {}, {}, {}, {}
    for h in heads:
        pair, half = divmod(h, 2)
        qp[h] = q_ref[:, 128 * pair:128 * (pair + 1)]
        kp = k_ref[:, 128 * pair:128 * (pair + 1)]
        in_head = (lane >= RET_KEY_DIM) if half else (lane < RET_KEY_DIM)
        km[h] = jnp.where(in_head, kp, jnp.zeros_like(kp))
        v[h] = v_ref[:, 128 * h:128 * (h + 1)]
        s[h] = lax.dot_general(qp[h], km[h], (((1,), (1,)), ((), ())),
                               preferred_element_type=F32)
    for h in heads:
        state = state_scr[h]
        cross = jnp.dot(qp[h], state.astype(BF16), preferred_element_type=F32)
        intra = jnp.dot((s[h] * dmat_ref[h]).astype(BF16), v[h],
                        preferred_element_type=F32)
        outs[h] = intra + qdec_ref[h] * cross
        vd = (v[h].astype(F32) * kdec_ref[h]).astype(BF16)
        state_scr[h] = cdec_ref[h] * state + lax.dot_general(
            km[h], vd, (((0,), (0,)), ((), ())), preferred_element_type=F32)
    for h in heads:
        o = outs[h]
        mu = jnp.mean(o, axis=-1, keepdims=True)
        d = o - mu
        var = jnp.mean(d * d, axis=-1, keepdims=True)
        y = d * lax.rsqrt(var + EPS) * gn_ref[:, 128 * h:128 * (h + 1)]
        gate = gate_ref[:, 128 * h:128 * (h + 1)].astype(F32)
        y = y * (gate * (1.0 / (1.0 + jnp.exp(-gate))))
        o_ref[:, 128 * h:128 * (h + 1)] = y.astype(o_ref.dtype)


def _ret_tables():
    T = T_RET
    log_g = np.log1p(-np.exp2(-5.0 - np.arange(RET_HEADS, dtype=np.float64)))
    n = np.arange(T, dtype=np.float64)
    kscale = RET_KEY_DIM ** -0.5
    allowed = (np.arange(T)[None, :] // CHUNK) <= (np.arange(T)[:, None] // CHUNK)
    dmat = np.where(allowed[None],
                    np.exp(log_g[:, None, None] * np.abs(n[:, None] - n[None, :])[None]), 0.0) * kscale
    qdec = np.exp(log_g[:, None] * (n + 1.0)[None]) * kscale
    kdec = np.exp(log_g[:, None] * (T - 1.0 - n)[None])
    cdec = np.exp(log_g * T)
    qdec = np.broadcast_to(qdec[:, :, None], (RET_HEADS, T, 128))
    kdec = np.broadcast_to(kdec[:, :, None], (RET_HEADS, T, 128))
    cdec = np.broadcast_to(cdec[:, None, None], (RET_HEADS, 1, 128))
    return tuple(jnp.asarray(a, F32) for a in (dmat, qdec, kdec, cdec))


def _retention(proj, batch, seq, gn_g):
    T = T_RET
    nt = seq // T
    dmat, qdec, kdec, cdec = _ret_tables()
    const = lambda shape: pl.BlockSpec(shape, lambda b, t: (0,) * len(shape))
    return pl.pallas_call(
        _ret_kernel,
        grid=(batch, nt),
        in_specs=[
            pl.BlockSpec((T, 256), lambda b, t: (b * nt + t, COLB_RQ)),
            pl.BlockSpec((T, 256), lambda b, t: (b * nt + t, COLB_RK)),
            pl.BlockSpec((T, 512), lambda b, t: (b * nt + t, COLB_RV)),
            pl.BlockSpec((T, 512), lambda b, t: (b * nt + t, COLB_RG)),
            const((RET_HEADS, T, T)), const((RET_HEADS, T, 128)),
            const((RET_HEADS, T, 128)), const((RET_HEADS, 1, 128)),
            const((1, RET_WIDTH)),
        ],
        out_specs=pl.BlockSpec((T, RET_WIDTH), lambda b, t: (b * nt + t, 0)),
        out_shape=jax.ShapeDtypeStruct((batch * seq, RET_WIDTH), BF16),
        scratch_shapes=[pltpu.VMEM((RET_HEADS, 128, 128), F32)],
        compiler_params=pltpu.CompilerParams(
            dimension_semantics=("arbitrary", "arbitrary"), vmem_limit_bytes=VMEM_LIMIT),
        name="retention",
    )(proj, proj, proj, proj, dmat, qdec, kdec, cdec, gn_g)


def _ffn_kernel(tiles_per_seq, x_ref, od_ref, or_ref, wo_ref, g2_ref, wup_ref,
                cw_ref, cb_ref, wd_ref, gf_ref, o_ref, xn_scr, h_scr, halo_scr):
    TM = TM_FFN

    @pl.when(pl.program_id(0) % tiles_per_seq == 0)
    def _():
        halo_scr[...] = jnp.zeros(halo_scr.shape, F32)

    x1 = (x_ref[...]
          + jnp.dot(od_ref[...], wo_ref[0:DIFF_WIDTH, :], preferred_element_type=F32)
          + jnp.dot(or_ref[...], wo_ref[DIFF_WIDTH:, :], preferred_element_type=F32))
    o_ref[...] = x1
    xn_scr[...] = _rms(x1, g2_ref[...]).astype(BF16)

    row = lax.broadcasted_iota(jnp.int32, (TM, TF), 0)

    for c in range(NF):
        xn = xn_scr[...]
        cols = slice(c * TF, (c + 1) * TF)
        gate_cols = slice(D_FF + c * TF, D_FF + (c + 1) * TF)
        a = jnp.dot(xn, wup_ref[:, cols], preferred_element_type=F32)
        b = jnp.dot(xn, wup_ref[:, gate_cols], preferred_element_type=F32)
        halo = halo_scr[c]
        halo_scr[c] = a[TM - 8:TM, :]
        a1 = jnp.where(row == 0, halo[7:8, :], pltpu.roll(a, 1, 0))
        a2 = pltpu.roll(a, 2, 0)
        a2 = jnp.where(row == 0, halo[6:7, :], jnp.where(row == 1, halo[7:8, :], a2))
        u = (cw_ref[0:1, cols] * a2 + cw_ref[1:2, cols] * a1 + cw_ref[2:3, cols] * a
             + cb_ref[:, cols])
        gelu = 0.5 * u * (1.0 + jnp.tanh(0.7978845608028654 * (u + 0.044715 * (u * u * u))))
        h_scr[:, cols] = (gelu * b).astype(BF16)

    y = o_ref[...] + jnp.dot(h_scr[...], wd_ref[...], preferred_element_type=F32)
    o_ref[...] = _rms(y, gf_ref[...])


def _ffn(x2, o_diff, o_ret, wo, g2, wup, cw, cb, wd, gf, seq):
    n = x2.shape[0]
    TM = TM_FFN
    const = lambda shape: pl.BlockSpec(shape, lambda i: (0,) * len(shape),
                                       pipeline_mode=pl.Buffered(1))
    return pl.pallas_call(
        functools.partial(_ffn_kernel, seq // TM),
        grid=(n // TM,),
        in_specs=[
            pl.BlockSpec((TM, D_MODEL), lambda i: (i, 0)),
            pl.BlockSpec((TM, DIFF_WIDTH), lambda i: (i, 0)),
            pl.BlockSpec((TM, RET_WIDTH), lambda i: (i, 0)),
            const((D_MODEL, D_MODEL)), const((1, D_MODEL)),
            const((D_MODEL, 2 * D_FF)),
            const((3, D_FF)), const((1, D_FF)),
            const((D_FF, D_MODEL)), const((1, D_MODEL)),
        ],
        out_specs=pl.BlockSpec((TM, D_MODEL), lambda i: (i, 0)),
        out_shape=jax.ShapeDtypeStruct((n, D_MODEL), F32),
        scratch_shapes=[
            pltpu.VMEM((TM, D_MODEL), BF16),
            pltpu.VMEM((TM, D_FF), BF16),
            pltpu.VMEM((NF, 8, TF), F32),
        ],
        compiler_params=pltpu.CompilerParams(
            dimension_semantics=("arbitrary",), vmem_limit_bytes=VMEM_LIMIT),
        name="outproj_ffn",
    )(x2, o_diff, o_ret, wo, g2, wup, cw, cb, wd, gf)


def kernel(x, norm_mix_g, w_in, lambda_q1, lambda_k1, lambda_q2, lambda_k2, diff_subln_g,
           ret_gn_g, w_out, norm_ffn_g, w_up, conv_w, conv_b, w_down, final_norm_g):
    batch, seq, _ = x.shape
    x2 = x.reshape(batch * seq, D_MODEL)

    proj = _inproj(x2, norm_mix_g, w_in[0].astype(BF16))
    o_diff = _diff_attention(proj, batch, seq, lambda_q1, lambda_k1, lambda_q2, lambda_k2,
                             diff_subln_g)
    o_ret = _retention(proj, batch, seq, ret_gn_g)

    y = _ffn(x2, o_diff, o_ret, w_out[0].astype(BF16), norm_ffn_g, w_up[0].astype(BF16),
             conv_w[0], conv_b[0].reshape(1, D_FF), w_down[0].astype(BF16),
             final_norm_g.reshape(1, D_MODEL), seq)
    return y.reshape(batch, seq, D_MODEL)
```

```python
import functools
import math

import jax
import jax.numpy as jnp
import numpy as np
from jax import lax
from jax.experimental import pallas as pl
from jax.experimental.pallas import tpu as pltpu

D_MODEL = 1024
CHUNK = 64
DIFF_HEADS = 4
DIFF_HEAD_DIM = 64
DIFF_V_DIM = 128
DIFF_WIDTH = 512
RET_HEADS = 4
RET_KEY_DIM = 64
RET_V_DIM = 128
RET_WIDTH = 512
D_FF = 2816
IN_WIDTH = 3072
EPS = 1e-6
LAM_INIT = 0.8 - 0.6 * math.exp(-0.3 * 0)
LOG2E = math.log2(math.e)
MASKED = -1e30

COL_DQ, COL_DK, COL_DV = 0, 4, 8
COLB_RQ, COLB_RK = 6, 7
COLB_RV, COLB_RG = 4, 5

TM_IN = 1024
TQ_ATT = 512
TK_ATT = 512
V_ROWS = DIFF_V_DIM + 16
EXP2_UNDERFLOW = 150.0
BOUND_MARGIN = 2.0
NORM_SLACK = 1.001
KNORM_SLACK = 1.002
T_RET = 256
RET_BLOCKS = 4
TM_FFN = 1024
TF = 256
NF = D_FF // TF
VMEM_LIMIT = 56 * 1024 * 1024

BF16 = jnp.bfloat16
F32 = jnp.float32


def _rms(x, g):
    ms = jnp.mean(x * x, axis=-1, keepdims=True)
    return x * lax.rsqrt(ms + EPS) * g


def _inproj_kernel(x_ref, g_ref, w_ref, o_ref):
    h = _rms(x_ref[...], g_ref[...]).astype(BF16)
    o_ref[...] = jnp.dot(h, w_ref[...], preferred_element_type=F32).astype(o_ref.dtype)


def _inproj(x2, g, w_bf16):
    n = x2.shape[0]
    return pl.pallas_call(
        _inproj_kernel,
        grid=(n // TM_IN,),
        in_specs=[
            pl.BlockSpec((TM_IN, D_MODEL), lambda i: (i, 0)),
            pl.BlockSpec((1, D_MODEL), lambda i: (0, 0)),
            pl.BlockSpec((D_MODEL, IN_WIDTH), lambda i: (0, 0), pipeline_mode=pl.Buffered(1)),
        ],
        out_specs=pl.BlockSpec((TM_IN, IN_WIDTH), lambda i: (i, 0)),
        out_shape=jax.ShapeDtypeStruct((n, IN_WIDTH), BF16),
        compiler_params=pltpu.CompilerParams(
            dimension_semantics=("arbitrary",), vmem_limit_bytes=VMEM_LIMIT),
        name="inproj",
    )(x2, g, w_bf16)


def _attn_kernel(q_ref, k_ref, v_ref, postab_ref, ctab_ref, cpos_ref, corr_ref,
                 lq1_ref, lk1_ref, lq2_ref, lk2_ref, g_ref, o_ref,
                 kaug_scr, vt_scr, qt_scr, first_scr, s_scr, m_scr, acc_scr):
    TQ, TK = TQ_ATT, TK_ATT
    DV, DH = DIFF_V_DIM, DIFF_HEAD_DIM
    seq = k_ref.shape[0]
    n_tiles = seq // TK
    tile_lane = lax.broadcasted_iota(jnp.int32, (1, 128), 1)
    lane = lax.broadcasted_iota(jnp.int32, (TK, 128), 1)
    ctab = ctab_ref[0]
    slope_end = cpos_ref[0, 0:1, :]
    slope = cpos_ref[0, 1:2, :]

    ones_row = lax.broadcasted_iota(jnp.int32, (128, 128), 0)
    ones_col = lax.broadcasted_iota(jnp.int32, (128, 128), 1)
    half_ones = jnp.where((ones_row < DH) == (ones_col < DH), 1.0, 0.0).astype(BF16)
    prefix_max = [jnp.zeros((1, 128), F32)] * 2
    running = [jnp.zeros((1, 1), F32)] * 2
    firsts = jnp.zeros((1, 128), jnp.int32)
    for c in range(n_tiles):
        rows = slice(c * TK, (c + 1) * TK)
        kc = k_ref[rows, :]
        pt = postab_ref[rows, :]
        kaug_scr[0, rows, :] = jnp.where(lane < DH, kc, pt)
        kaug_scr[1, rows, :] = jnp.where(lane >= DH, kc, pt)
        vt_scr[c, 0:DV, :] = v_ref[rows, :].T
        vt_scr[c, DV:, :] = jnp.ones((V_ROWS - DV, TK), BF16)
        ksq = kc.astype(F32)
        ksq = (ksq * ksq).astype(BF16)
        n2 = jnp.max(jnp.dot(ksq, half_ones, preferred_element_type=F32),
                     axis=0, keepdims=True)
        q = q_ref[rows, :].astype(F32) * (DH ** -0.5 * LOG2E)
        skip = tile_lane < c
        for half in range(2):
            in_half = (lane >= DH) if half else (lane < DH)
            in_half_row = (tile_lane >= DH) if half else (tile_lane < DH)
            k_norm = jnp.sqrt(jnp.max(jnp.where(in_half_row, n2, 0.0), axis=1,
                                      keepdims=True)) * KNORM_SLACK
            qb = jnp.where(in_half, q, ctab).astype(BF16).T
            qt_scr[c, half] = qb
            qf = qb[half * DH:(half + 1) * DH, :].astype(F32)
            q_norm = jnp.sqrt(jnp.max(jnp.sum(qf * qf, axis=0, keepdims=True),
                                      axis=1, keepdims=True)) * NORM_SLACK
            upper = q_norm * prefix_max[half] + slope_end
            lower = slope * float(c * TQ) - q_norm * k_norm
            skip = jnp.logical_and(skip, upper + BOUND_MARGIN < lower - EXP2_UNDERFLOW)
            running[half] = jnp.maximum(running[half], k_norm)
            prefix_max[half] = jnp.where(tile_lane == c, running[half], prefix_max[half])
        first_c = jnp.sum(skip.astype(jnp.int32), axis=1, keepdims=True)
        firsts = jnp.where(tile_lane == c, first_c, firsts)
    first_scr[...] = firsts
    acc_scr[...] = jnp.ones(acc_scr.shape, F32)

    lam = (jnp.exp(jnp.sum(lq1_ref[...] * lk1_ref[...], axis=1, keepdims=True))
           - jnp.exp(jnp.sum(lq2_ref[...] * lk2_ref[...], axis=1, keepdims=True))
           + LAM_INIT)

    def scores(i, j, slot, halves=(0, 1)):
        start = pl.multiple_of(j * TK, TK)
        for half in halves:
            s_scr[slot, half] = jnp.dot(kaug_scr[half, pl.ds(start, TK), :], qt_scr[i, half],
                                        preferred_element_type=F32)

    def consume(j, slot, corr, halves=(0, 1)):
        vt = vt_scr[j]
        for half in halves:
            s = s_scr[slot, half]
            if corr is not None:
                s = s + corr
            m_prev = m_scr[half]
            m_new = jnp.maximum(m_prev, jnp.max(s, axis=0, keepdims=True))
            alpha = jnp.exp2(m_prev - m_new)
            p = jnp.exp2(s - m_new).astype(BF16)
            acc_scr[half] = alpha * acc_scr[half] + jnp.dot(
                vt, p, preferred_element_type=F32)
            m_scr[half] = m_new

    def finish(i):
        inv_l0 = 1.0 / acc_scr[0, DV:DV + 1, :]
        inv_l1 = 1.0 / acc_scr[1, DV:DV + 1, :]
        o_t = acc_scr[0, 0:DV, :] * inv_l0 - lam * (acc_scr[1, 0:DV, :] * inv_l1)
        ms = jnp.mean(o_t * o_t, axis=0, keepdims=True)
        y = (o_t * lax.rsqrt(ms + EPS)).T
        row0 = pl.multiple_of(i * TQ, TQ)
        o_ref[pl.ds(row0, TQ), :] = (y * g_ref[...] * (1.0 - LAM_INIT)).astype(o_ref.dtype)

    n_q = seq // TQ

    def q_tile(i, carry):
        firsts_v = first_scr[...]
        nxt = jnp.minimum(i + 1, n_q - 1)
        first = jnp.sum(jnp.where(tile_lane == i, firsts_v, 0))
        first_nxt = jnp.sum(jnp.where(tile_lane == nxt, firsts_v, 0))
        n_past = i - first
        n_rem = n_past - 1

        def start():
            finish(jnp.maximum(i - 1, 0))
            m_scr[...] = jnp.full(m_scr.shape, MASKED, F32)
            acc_scr[...] = jnp.zeros(acc_scr.shape, F32)

        @pl.when(n_past == 0)
        def _():
            start()
            for half in range(2):
                consume(i, 2, corr_ref[0], (half,))
                scores(nxt, first_nxt, 2, (half,))

        @pl.when(n_past > 0)
        def _():
            scores(i, first + 1, 0)
            start()
            consume(first, 2, None)

        def tile_group(j, n):
            scores(i, j + 1, 1)
            for u in range(n - 1):
                for half in range(2):
                    consume(j + u, u % 2, None, (half,))
                    scores(i, j + u + 2, u % 2, (half,))
            consume(j + n - 1, (n - 1) % 2, None)

        def group_loop(base, trips, n):
            def body(t, c):
                tile_group(base + n * t, n)
                return c
            lax.fori_loop(0, trips, body, 0)

        n_loop = jnp.maximum(n_rem, 0)
        group_loop(first + 1, n_loop // 4, 4)
        group_loop(first + 1 + 4 * (n_loop // 4), (n_loop % 4) // 2, 2)

        @pl.when(jnp.logical_and(n_past > 0, n_rem % 2 == 0))
        def _():
            scores(nxt, first_nxt, 2)
            consume(i, 0, corr_ref[0])

        @pl.when(jnp.logical_and(n_past > 0, n_rem % 2 == 1))
        def _():
            scores(i, i, 1)
            consume(i - 1, 0, None)
            scores(nxt, first_nxt, 2)
            consume(i, 1, corr_ref[0])

        return carry

    scores(0, 0, 2)
    lax.fori_loop(0, n_q, q_tile, 0)
    finish(n_q - 1)


def _bf16_pieces(x, n):
    x = np.asarray(x, np.float32)
    pieces = []
    for _ in range(n):
        p = x.astype(BF16).astype(np.float32)
        pieces.append(p)
        x = x - p
    return pieces


def _attn_tables(seq):
    TQ, TK = TQ_ATT, TK_ATT
    slopes = (np.exp2(-8.0 * np.arange(1, DIFF_HEADS + 1, dtype=np.float64) / DIFF_HEADS)
              * LOG2E).astype(np.float32)
    c1, c2, c3 = _bf16_pieces(slopes, 3)
    ctab = np.zeros((DIFF_HEADS, 1, 128), np.float32)
    pos = np.arange(seq)
    p_hi, p_lo = (pos // 64) * 64, pos % 64
    postab = np.zeros((seq, 128), np.float32)
    for base in (0, DIFF_HEAD_DIM):
        for n, (cp, pp) in enumerate(((c1, p_hi), (c2, p_hi), (c3, p_hi),
                                      (c1, p_lo), (c2, p_lo), (c3, p_lo))):
            ctab[:, 0, base + n] = cp
            postab[:, base + n] = pp
    c = np.arange(TK)[:, None]
    r = np.arange(TQ)[None, :]
    allowed = (c // CHUNK) <= (r // CHUNK)
    ahead = np.where(c > r, 2.0 * (r - c), 0.0)
    corr = np.where(allowed[None], slopes.astype(np.float64)[:, None, None] * ahead[None], MASKED)
    cpos = np.zeros((DIFF_HEADS, 2, 128), np.float64)
    cpos[:, 0, :] = slopes.astype(np.float64)[:, None] * ((np.arange(128) + 1) * TK - 1)[None, :]
    cpos[:, 1, :] = slopes.astype(np.float64)[:, None]
    return (jnp.asarray(postab, BF16), jnp.asarray(ctab, F32), jnp.asarray(cpos, F32),
            jnp.asarray(corr, F32))


def _diff_attention(proj, batch, seq, lq1, lk1, lq2, lk2, subln_g):
    TQ, TK = TQ_ATT, TK_ATT
    assert TQ == TK and seq % TK == 0
    assert seq // TK <= 128
    postab, ctab, cpos, corr = _attn_tables(seq)
    small = lambda shape: pl.BlockSpec(shape, lambda b, h: (0,) * len(shape))
    return pl.pallas_call(
        _attn_kernel,
        grid=(batch, DIFF_HEADS),
        in_specs=[
            pl.BlockSpec((seq, 128), lambda b, h: (b, COL_DQ + h)),
            pl.BlockSpec((seq, 128), lambda b, h: (b, COL_DK + h)),
            pl.BlockSpec((seq, 128), lambda b, h: (b, COL_DV + h)),
            pl.BlockSpec((seq, 128), lambda b, h: (0, 0), pipeline_mode=pl.Buffered(1)),
            pl.BlockSpec((1, 1, 128), lambda b, h: (h, 0, 0)),
            pl.BlockSpec((1, 2, 128), lambda b, h: (h, 0, 0)),
            pl.BlockSpec((1, TK, TQ), lambda b, h: (h, 0, 0)),
            small((1, DIFF_HEAD_DIM)), small((1, DIFF_HEAD_DIM)),
            small((1, DIFF_HEAD_DIM)), small((1, DIFF_HEAD_DIM)),
            small((1, DIFF_V_DIM)),
        ],
        out_specs=pl.BlockSpec((seq, DIFF_V_DIM), lambda b, h: (b, h)),
        out_shape=jax.ShapeDtypeStruct((batch * seq, DIFF_WIDTH), BF16),
        scratch_shapes=[
            pltpu.VMEM((2, seq, 128), BF16),
            pltpu.VMEM((seq // TK, V_ROWS, TK), BF16),
            pltpu.VMEM((seq // TQ, 2, 128, TQ), BF16),
            pltpu.VMEM((1, 128), jnp.int32),
            pltpu.VMEM((3, 2, TK, TQ), F32),
            pltpu.VMEM((2, 1, TQ), F32),
            pltpu.VMEM((2, V_ROWS, TQ), F32),
        ],
        compiler_params=pltpu.CompilerParams(
            dimension_semantics=("arbitrary", "arbitrary"), vmem_limit_bytes=VMEM_LIMIT),
        name="diff_attn",
    )(proj, proj, proj, postab, ctab, cpos, corr, lq1, lk1, lq2, lk2, subln_g)


def _ret_kernel(q_ref, k_ref, v_ref, gate_ref, dmat_ref, qdec_ref, kdec_ref, cdec_ref,
                gn_ref, o_ref, state_scr):
    T = T_RET

    @pl.when(pl.program_id(1) == 0)
    def _():
        state_scr[...] = jnp.zeros(state_scr.shape, F32)

    lane = lax.broadcasted_iota(jnp.int32, (T, 128), 1)
    heads = range(RET_HEADS)
    for r in range(RET_BLOCKS):
        rows = slice(r * T, (r + 1) * T)
        qp, km, v, s, outs = {}, {}, {}, {}, {}
        for h in heads:
            pair, half = divmod(h, 2)
            qp[h] = q_ref[rows, 128 * pair:128 * (pair + 1)]
            kp = k_ref[rows, 128 * pair:128 * (pair + 1)]
            in_head = (lane >= RET_KEY_DIM) if half else (lane < RET_KEY_DIM)
            km[h] = jnp.where(in_head, kp, jnp.zeros_like(kp))
            v[h] = v_ref[rows, 128 * h:128 * (h + 1)]
            s[h] = lax.dot_general(qp[h], km[h], (((1,), (1,)), ((), ())),
                                   preferred_element_type=F32)
        for h in heads:
            state = state_scr[h]
            cross = jnp.dot(qp[h], state.astype(BF16), preferred_element_type=F32)
            intra = jnp.dot((s[h] * dmat_ref[h]).astype(BF16), v[h],
                            preferred_element_type=F32)
            outs[h] = intra + qdec_ref[h] * cross
            vd = (v[h].astype(F32) * kdec_ref[h]).astype(BF16)
            state_scr[h] = cdec_ref[h] * state + lax.dot_general(
                km[h], vd, (((0,), (0,)), ((), ())), preferred_element_type=F32)
        for h in heads:
            cols = slice(128 * h, 128 * (h + 1))
            o = outs[h]
            mu = jnp.mean(o, axis=-1, keepdims=True)
            d = o - mu
            var = jnp.mean(d * d, axis=-1, keepdims=True)
            y = d * lax.rsqrt(var + EPS) * gn_ref[:, cols]
            gate = gate_ref[rows, cols].astype(F32)
            y = y * (gate * (1.0 / (1.0 + jnp.exp(-gate))))
            o_ref[rows, cols] = y.astype(o_ref.dtype)


def _ret_tables():
    T = T_RET
    log_g = np.log1p(-np.exp2(-5.0 - np.arange(RET_HEADS, dtype=np.float64)))
    n = np.arange(T, dtype=np.float64)
    kscale = RET_KEY_DIM ** -0.5
    allowed = (np.arange(T)[None, :] // CHUNK) <= (np.arange(T)[:, None] // CHUNK)
    dmat = np.where(allowed[None],
                    np.exp(log_g[:, None, None] * np.abs(n[:, None] - n[None, :])[None]), 0.0) * kscale
    qdec = np.exp(log_g[:, None] * (n + 1.0)[None]) * kscale
    kdec = np.exp(log_g[:, None] * (T - 1.0 - n)[None])
    cdec = np.exp(log_g * T)
    qdec = np.broadcast_to(qdec[:, :, None], (RET_HEADS, T, 128))
    kdec = np.broadcast_to(kdec[:, :, None], (RET_HEADS, T, 128))
    cdec = np.broadcast_to(cdec[:, None, None], (RET_HEADS, 1, 128))
    return tuple(jnp.asarray(a, F32) for a in (dmat, qdec, kdec, cdec))


def _retention(proj, batch, seq, gn_g):
    T = T_RET
    TS = T * RET_BLOCKS
    nt = seq // TS
    dmat, qdec, kdec, cdec = _ret_tables()
    const = lambda shape: pl.BlockSpec(shape, lambda b, t: (0,) * len(shape))
    return pl.pallas_call(
        _ret_kernel,
        grid=(batch, nt),
        in_specs=[
            pl.BlockSpec((TS, 256), lambda b, t: (b * nt + t, COLB_RQ)),
            pl.BlockSpec((TS, 256), lambda b, t: (b * nt + t, COLB_RK)),
            pl.BlockSpec((TS, 512), lambda b, t: (b * nt + t, COLB_RV)),
            pl.BlockSpec((TS, 512), lambda b, t: (b * nt + t, COLB_RG)),
            const((RET_HEADS, T, T)), const((RET_HEADS, T, 128)),
            const((RET_HEADS, T, 128)), const((RET_HEADS, 1, 128)),
            const((1, RET_WIDTH)),
        ],
        out_specs=pl.BlockSpec((TS, RET_WIDTH), lambda b, t: (b * nt + t, 0)),
        out_shape=jax.ShapeDtypeStruct((batch * seq, RET_WIDTH), BF16),
        scratch_shapes=[pltpu.VMEM((RET_HEADS, 128, 128), F32)],
        compiler_params=pltpu.CompilerParams(
            dimension_semantics=("arbitrary", "arbitrary"), vmem_limit_bytes=VMEM_LIMIT),
        name="retention",
    )(proj, proj, proj, proj, dmat, qdec, kdec, cdec, gn_g)


def _ffn_kernel(tiles_per_seq, x_ref, od_ref, or_ref, wo_ref, g2_ref, wup_ref,
                cw_ref, cb_ref, wd_ref, gf_ref, o_ref, xn_scr, h_scr, halo_scr):
    TM = TM_FFN

    @pl.when(pl.program_id(0) % tiles_per_seq == 0)
    def _():
        halo_scr[...] = jnp.zeros(halo_scr.shape, F32)

    x1 = (x_ref[...]
          + jnp.dot(od_ref[...], wo_ref[0:DIFF_WIDTH, :], preferred_element_type=F32)
          + jnp.dot(or_ref[...], wo_ref[DIFF_WIDTH:, :], preferred_element_type=F32))
    o_ref[...] = x1
    xn_scr[...] = _rms(x1, g2_ref[...]).astype(BF16)

    row = lax.broadcasted_iota(jnp.int32, (TM, TF), 0)

    for c in range(NF):
        xn = xn_scr[...]
        cols = slice(c * TF, (c + 1) * TF)
        gate_cols = slice(D_FF + c * TF, D_FF + (c + 1) * TF)
        a = jnp.dot(xn, wup_ref[:, cols], preferred_element_type=F32)
        b = jnp.dot(xn, wup_ref[:, gate_cols], preferred_element_type=F32)
        halo = halo_scr[c]
        halo_scr[c] = a[TM - 8:TM, :]
        a1 = jnp.where(row == 0, halo[7:8, :], pltpu.roll(a, 1, 0))
        a2 = pltpu.roll(a, 2, 0)
        a2 = jnp.where(row == 0, halo[6:7, :], jnp.where(row == 1, halo[7:8, :], a2))
        u = (cw_ref[0:1, cols] * a2 + cw_ref[1:2, cols] * a1 + cw_ref[2:3, cols] * a
             + cb_ref[:, cols])
        gelu = 0.5 * u * (1.0 + jnp.tanh(0.7978845608028654 * (u + 0.044715 * (u * u * u))))
        h_scr[:, cols] = (gelu * b).astype(BF16)

    y = o_ref[...] + jnp.dot(h_scr[...], wd_ref[...], preferred_element_type=F32)
    o_ref[...] = _rms(y, gf_ref[...])


def _ffn(x2, o_diff, o_ret, wo, g2, wup, cw, cb, wd, gf, seq):
    n = x2.shape[0]
    TM = TM_FFN
    const = lambda shape: pl.BlockSpec(shape, lambda i: (0,) * len(shape),
                                       pipeline_mode=pl.Buffered(1))
    return pl.pallas_call(
        functools.partial(_ffn_kernel, seq // TM),
        grid=(n // TM,),
        in_specs=[
            pl.BlockSpec((TM, D_MODEL), lambda i: (i, 0)),
            pl.BlockSpec((TM, DIFF_WIDTH), lambda i: (i, 0)),
            pl.BlockSpec((TM, RET_WIDTH), lambda i: (i, 0)),
            const((D_MODEL, D_MODEL)), const((1, D_MODEL)),
            const((D_MODEL, 2 * D_FF)),
            const((3, D_FF)), const((1, D_FF)),
            const((D_FF, D_MODEL)), const((1, D_MODEL)),
        ],
        out_specs=pl.BlockSpec((TM, D_MODEL), lambda i: (i, 0)),
        out_shape=jax.ShapeDtypeStruct((n, D_MODEL), F32),
        scratch_shapes=[
            pltpu.VMEM((TM, D_MODEL), BF16),
            pltpu.VMEM((TM, D_FF), BF16),
            pltpu.VMEM((NF, 8, TF), F32),
        ],
        compiler_params=pltpu.CompilerParams(
            dimension_semantics=("arbitrary",), vmem_limit_bytes=VMEM_LIMIT),
        name="outproj_ffn",
    )(x2, o_diff, o_ret, wo, g2, wup, cw, cb, wd, gf)


def kernel(x, norm_mix_g, w_in, lambda_q1, lambda_k1, lambda_q2, lambda_k2, diff_subln_g,
           ret_gn_g, w_out, norm_ffn_g, w_up, conv_w, conv_b, w_down, final_norm_g):
    batch, seq, _ = x.shape
    x2 = x.reshape(batch * seq, D_MODEL)

    proj = _inproj(x2, norm_mix_g, w_in[0].astype(BF16))
    o_diff = _diff_attention(proj, batch, seq, lambda_q1, lambda_k1, lambda_q2, lambda_k2,
                             diff_subln_g)
    o_ret = _retention(proj, batch, seq, ret_gn_g)

    y = _ffn(x2, o_diff, o_ret, w_out[0].astype(BF16), norm_ffn_g, w_up[0].astype(BF16),
             conv_w[0], conv_b[0].reshape(1, D_FF), w_down[0].astype(BF16),
             final_norm_g.reshape(1, D_MODEL), seq)
    return y.reshape(batch, seq, D_MODEL)
```

```python
import functools
import math

import jax
import jax.numpy as jnp
import numpy as np
from jax import lax
from jax.experimental import pallas as pl
from jax.experimental.pallas import tpu as pltpu

D_MODEL = 1024
CHUNK = 64
DIFF_HEADS = 4
DIFF_HEAD_DIM = 64
DIFF_V_DIM = 128
DIFF_WIDTH = 512
RET_HEADS = 4
RET_KEY_DIM = 64
RET_V_DIM = 128
RET_WIDTH = 512
D_FF = 2816
IN_WIDTH = 3072
EPS = 1e-6
LAM_INIT = 0.8 - 0.6 * math.exp(-0.3 * 0)
LOG2E = math.log2(math.e)
MASKED = -1e30

COL_DQ, COL_DK, COL_DV = 0, 4, 8
COLB_RQ, COLB_RK = 6, 7
COLB_RV, COLB_RG = 4, 5

TM_IN = 1024
TQ_ATT = 512
TK_ATT = 512
V_ROWS = DIFF_V_DIM + 16
EXP2_UNDERFLOW = 150.0
BOUND_MARGIN = 2.0
NORM_SLACK = 1.001
KNORM_SLACK = 1.002
T_RET = 256
RET_BLOCKS = 8
TM_FFN = 1024
TF = 256
NF = D_FF // TF
VMEM_LIMIT = 56 * 1024 * 1024

BF16 = jnp.bfloat16
F32 = jnp.float32


def _rms(x, g):
    ms = jnp.mean(x * x, axis=-1, keepdims=True)
    return x * lax.rsqrt(ms + EPS) * g


def _inproj_kernel(x_ref, g_ref, w_ref, o_ref):
    h = _rms(x_ref[...], g_ref[...]).astype(BF16)
    o_ref[...] = jnp.dot(h, w_ref[...], preferred_element_type=F32).astype(o_ref.dtype)


def _inproj(x2, g, w_bf16):
    n = x2.shape[0]
    return pl.pallas_call(
        _inproj_kernel,
        grid=(n // TM_IN,),
        in_specs=[
            pl.BlockSpec((TM_IN, D_MODEL), lambda i: (i, 0)),
            pl.BlockSpec((1, D_MODEL), lambda i: (0, 0)),
            pl.BlockSpec((D_MODEL, IN_WIDTH), lambda i: (0, 0), pipeline_mode=pl.Buffered(1)),
        ],
        out_specs=pl.BlockSpec((TM_IN, IN_WIDTH), lambda i: (i, 0)),
        out_shape=jax.ShapeDtypeStruct((n, IN_WIDTH), BF16),
        compiler_params=pltpu.CompilerParams(
            dimension_semantics=("arbitrary",), vmem_limit_bytes=VMEM_LIMIT),
        name="inproj",
    )(x2, g, w_bf16)


def _attn_kernel(q_ref, k_ref, v_ref, postab_ref, ctab_ref, cpos_ref, corr_ref,
                 lq1_ref, lk1_ref, lq2_ref, lk2_ref, g_ref, o_ref,
                 kaug_scr, vt_scr, qt_scr, first_scr, nxt_scr, s_scr, m_scr, acc_scr):
    TQ, TK = TQ_ATT, TK_ATT
    DV, DH = DIFF_V_DIM, DIFF_HEAD_DIM
    seq = k_ref.shape[0]
    n_tiles = seq // TK
    tile_lane = lax.broadcasted_iota(jnp.int32, (1, 128), 1)
    lane = lax.broadcasted_iota(jnp.int32, (TK, 128), 1)
    ctab = ctab_ref[0]
    slope_end = cpos_ref[0, 0:1, :]
    slope = cpos_ref[0, 1:2, :]

    ones_row = lax.broadcasted_iota(jnp.int32, (128, 128), 0)
    ones_col = lax.broadcasted_iota(jnp.int32, (128, 128), 1)
    half_ones = jnp.where((ones_row < DH) == (ones_col < DH), 1.0, 0.0).astype(BF16)
    prefix_max = [jnp.zeros((1, 128), F32)] * 2
    running = [jnp.zeros((1, 1), F32)] * 2
    firsts = jnp.zeros((1, 128), jnp.int32)
    for c in range(n_tiles):
        rows = slice(c * TK, (c + 1) * TK)
        kc = k_ref[rows, :]
        pt = postab_ref[rows, :]
        kaug_scr[0, rows, :] = jnp.where(lane < DH, kc, pt)
        kaug_scr[1, rows, :] = jnp.where(lane >= DH, kc, pt)
        vt_scr[c, 0:DV, :] = v_ref[rows, :].T
        vt_scr[c, DV:, :] = jnp.ones((V_ROWS - DV, TK), BF16)
        ksq = kc.astype(F32)
        ksq = (ksq * ksq).astype(BF16)
        n2 = jnp.max(jnp.dot(ksq, half_ones, preferred_element_type=F32),
                     axis=0, keepdims=True)
        q = q_ref[rows, :].astype(F32) * (DH ** -0.5 * LOG2E)
        skip = tile_lane < c
        for half in range(2):
            in_half = (lane >= DH) if half else (lane < DH)
            in_half_row = (tile_lane >= DH) if half else (tile_lane < DH)
            k_norm = jnp.sqrt(jnp.max(jnp.where(in_half_row, n2, 0.0), axis=1,
                                      keepdims=True)) * KNORM_SLACK
            qb = jnp.where(in_half, q, ctab).astype(BF16).T
            qt_scr[c, half] = qb
            qf = qb[half * DH:(half + 1) * DH, :].astype(F32)
            q_norm = jnp.sqrt(jnp.max(jnp.sum(qf * qf, axis=0, keepdims=True),
                                      axis=1, keepdims=True)) * NORM_SLACK
            upper = q_norm * prefix_max[half] + slope_end
            lower = slope * float(c * TQ) - q_norm * k_norm
            skip = jnp.logical_and(skip, upper + BOUND_MARGIN < lower - EXP2_UNDERFLOW)
            running[half] = jnp.maximum(running[half], k_norm)
            prefix_max[half] = jnp.where(tile_lane == c, running[half], prefix_max[half])
        first_c = jnp.sum(skip.astype(jnp.int32), axis=1, keepdims=True)
        firsts = jnp.where(tile_lane == c, first_c, firsts)
    first_scr[...] = firsts
    acc_scr[...] = jnp.ones(acc_scr.shape, F32)

    lam = (jnp.exp(jnp.sum(lq1_ref[...] * lk1_ref[...], axis=1, keepdims=True))
           - jnp.exp(jnp.sum(lq2_ref[...] * lk2_ref[...], axis=1, keepdims=True))
           + LAM_INIT)

    def scores(i, j, slot, halves=(0, 1)):
        start = pl.multiple_of(j * TK, TK)
        for half in halves:
            s_scr[slot, half] = jnp.dot(kaug_scr[half, pl.ds(start, TK), :], qt_scr[i, half],
                                        preferred_element_type=F32)

    def consume(j, slot, corr, halves=(0, 1)):
        vt = vt_scr[j]
        for half in halves:
            s = s_scr[slot, half]
            if corr is not None:
                s = s + corr
            m_prev = m_scr[half]
            m_new = jnp.maximum(m_prev, jnp.max(s, axis=0, keepdims=True))
            alpha = jnp.exp2(m_prev - m_new)
            p = jnp.exp2(s - m_new).astype(BF16)
            acc_scr[half] = alpha * acc_scr[half] + jnp.dot(
                vt, p, preferred_element_type=F32)
            m_scr[half] = m_new

    def finish(i):
        inv_l0 = 1.0 / acc_scr[0, DV:DV + 1, :]
        inv_l1 = 1.0 / acc_scr[1, DV:DV + 1, :]
        o_t = acc_scr[0, 0:DV, :] * inv_l0 - lam * (acc_scr[1, 0:DV, :] * inv_l1)
        ms = jnp.mean(o_t * o_t, axis=0, keepdims=True)
        y = (o_t * lax.rsqrt(ms + EPS)).T
        row0 = pl.multiple_of(i * TQ, TQ)
        o_ref[pl.ds(row0, TQ), :] = (y * g_ref[...] * (1.0 - LAM_INIT)).astype(o_ref.dtype)

    n_q = seq // TQ

    def q_tile(i, first):
        nxt = jnp.minimum(i + 1, n_q - 1)
        n_past = i - first
        n_rem = n_past - 1

        def start():
            nxt_scr[0] = jnp.sum(jnp.where(tile_lane == nxt, first_scr[...], 0))
            finish(jnp.maximum(i - 1, 0))
            m_scr[...] = jnp.full(m_scr.shape, MASKED, F32)
            acc_scr[...] = jnp.zeros(acc_scr.shape, F32)

        @pl.when(n_past == 0)
        def _():
            start()
            for half in range(2):
                consume(i, 2, corr_ref[0], (half,))
                scores(nxt, nxt_scr[0], 2, (half,))

        @pl.when(n_past > 0)
        def _():
            scores(i, first + 1, 0)
            start()
            consume(first, 2, None)

        def tile_group(j, n):
            scores(i, j + 1, 1)
            for u in range(n - 1):
                for half in range(2):
                    consume(j + u, u % 2, None, (half,))
                    scores(i, j + u + 2, u % 2, (half,))
            consume(j + n - 1, (n - 1) % 2, None)

        def group_loop(base, trips, n):
            def body(t, c):
                tile_group(base + n * t, n)
                return c
            lax.fori_loop(0, trips, body, 0)

        first_nxt = nxt_scr[0]
        n_loop = jnp.maximum(n_rem, 0)
        group_loop(first + 1, n_loop // 4, 4)
        group_loop(first + 1 + 4 * (n_loop // 4), (n_loop % 4) // 2, 2)

        @pl.when(jnp.logical_and(n_past > 0, n_rem % 2 == 0))
        def _():
            scores(nxt, first_nxt, 2)
            consume(i, 0, corr_ref[0])

        @pl.when(jnp.logical_and(n_past > 0, n_rem % 2 == 1))
        def _():
            scores(i, i, 1)
            consume(i - 1, 0, None)
            scores(nxt, first_nxt, 2)
            consume(i, 1, corr_ref[0])

        return first_nxt

    scores(0, 0, 2)
    lax.fori_loop(0, n_q, q_tile, jnp.int32(0))
    finish(n_q - 1)


def _bf16_pieces(x, n):
    x = np.asarray(x, np.float32)
    pieces = []
    for _ in range(n):
        p = x.astype(BF16).astype(np.float32)
        pieces.append(p)
        x = x - p
    return pieces


def _attn_tables(seq):
    TQ, TK = TQ_ATT, TK_ATT
    slopes = (np.exp2(-8.0 * np.arange(1, DIFF_HEADS + 1, dtype=np.float64) / DIFF_HEADS)
              * LOG2E).astype(np.float32)
    c1, c2, c3 = _bf16_pieces(slopes, 3)
    ctab = np.zeros((DIFF_HEADS, 1, 128), np.float32)
    pos = np.arange(seq)
    p_hi, p_lo = (pos // 64) * 64, pos % 64
    postab = np.zeros((seq, 128), np.float32)
    for base in (0, DIFF_HEAD_DIM):
        for n, (cp, pp) in enumerate(((c1, p_hi), (c2, p_hi), (c3, p_hi),
                                      (c1, p_lo), (c2, p_lo), (c3, p_lo))):
            ctab[:, 0, base + n] = cp
            postab[:, base + n] = pp
    c = np.arange(TK)[:, None]
    r = np.arange(TQ)[None, :]
    allowed = (c // CHUNK) <= (r // CHUNK)
    ahead = np.where(c > r, 2.0 * (r - c), 0.0)
    corr = np.where(allowed[None], slopes.astype(np.float64)[:, None, None] * ahead[None], MASKED)
    cpos = np.zeros((DIFF_HEADS, 2, 128), np.float64)
    cpos[:, 0, :] = slopes.astype(np.float64)[:, None] * ((np.arange(128) + 1) * TK - 1)[None, :]
    cpos[:, 1, :] = slopes.astype(np.float64)[:, None]
    return (jnp.asarray(postab, BF16), jnp.asarray(ctab, F32), jnp.asarray(cpos, F32),
            jnp.asarray(corr, F32))


def _diff_attention(proj, batch, seq, lq1, lk1, lq2, lk2, subln_g):
    TQ, TK = TQ_ATT, TK_ATT
    assert TQ == TK and seq % TK == 0
    assert seq // TK <= 128
    postab, ctab, cpos, corr = _attn_tables(seq)
    small = lambda shape: pl.BlockSpec(shape, lambda b, h: (0,) * len(shape))
    return pl.pallas_call(
        _attn_kernel,
        grid=(batch, DIFF_HEADS),
        in_specs=[
            pl.BlockSpec((seq, 128), lambda b, h: (b, COL_DQ + h)),
            pl.BlockSpec((seq, 128), lambda b, h: (b, COL_DK + h)),
            pl.BlockSpec((seq, 128), lambda b, h: (b, COL_DV + h)),
            pl.BlockSpec((seq, 128), lambda b, h: (0, 0), pipeline_mode=pl.Buffered(1)),
            pl.BlockSpec((1, 1, 128), lambda b, h: (h, 0, 0)),
            pl.BlockSpec((1, 2, 128), lambda b, h: (h, 0, 0)),
            pl.BlockSpec((1, TK, TQ), lambda b, h: (h, 0, 0)),
            small((1, DIFF_HEAD_DIM)), small((1, DIFF_HEAD_DIM)),
            small((1, DIFF_HEAD_DIM)), small((1, DIFF_HEAD_DIM)),
            small((1, DIFF_V_DIM)),
        ],
        out_specs=pl.BlockSpec((seq, DIFF_V_DIM), lambda b, h: (b, h)),
        out_shape=jax.ShapeDtypeStruct((batch * seq, DIFF_WIDTH), BF16),
        scratch_shapes=[
            pltpu.VMEM((2, seq, 128), BF16),
            pltpu.VMEM((seq // TK, V_ROWS, TK), BF16),
            pltpu.VMEM((seq // TQ, 2, 128, TQ), BF16),
            pltpu.VMEM((1, 128), jnp.int32),
            pltpu.SMEM((1,), jnp.int32),
            pltpu.VMEM((3, 2, TK, TQ), F32),
            pltpu.VMEM((2, 1, TQ), F32),
            pltpu.VMEM((2, V_ROWS, TQ), F32),
        ],
        compiler_params=pltpu.CompilerParams(
            dimension_semantics=("arbitrary", "arbitrary"), vmem_limit_bytes=VMEM_LIMIT),
        name="diff_attn",
    )(proj, proj, proj, postab, ctab, cpos, corr, lq1, lk1, lq2, lk2, subln_g)


def _ret_kernel(q_ref, k_ref, v_ref, gate_ref, dmat_ref, qdec_ref, kdec_ref, cdec_ref,
                gn_ref, o_ref, state_scr):
    T = T_RET

    @pl.when(pl.program_id(1) == 0)
    def _():
        state_scr[...] = jnp.zeros(state_scr.shape, F32)

    lane = lax.broadcasted_iota(jnp.int32, (T, 128), 1)
    heads = range(RET_HEADS)
    for r in range(RET_BLOCKS):
        rows = slice(r * T, (r + 1) * T)
        qp, km, v, s, outs = {}, {}, {}, {}, {}
        for h in heads:
            pair, half = divmod(h, 2)
            qp[h] = q_ref[rows, 128 * pair:128 * (pair + 1)]
            kp = k_ref[rows, 128 * pair:128 * (pair + 1)]
            in_head = (lane >= RET_KEY_DIM) if half else (lane < RET_KEY_DIM)
            km[h] = jnp.where(in_head, kp, jnp.zeros_like(kp))
            v[h] = v_ref[rows, 128 * h:128 * (h + 1)]
            s[h] = lax.dot_general(qp[h], km[h], (((1,), (1,)), ((), ())),
                                   preferred_element_type=F32)
        for h in heads:
            state = state_scr[h]
            cross = jnp.dot(qp[h], state.astype(BF16), preferred_element_type=F32)
            intra = jnp.dot((s[h] * dmat_ref[h]).astype(BF16), v[h],
                            preferred_element_type=F32)
            outs[h] = intra + qdec_ref[h] * cross
            vd = (v[h].astype(F32) * kdec_ref[h]).astype(BF16)
            state_scr[h] = cdec_ref[h] * state + lax.dot_general(
                km[h], vd, (((0,), (0,)), ((), ())), preferred_element_type=F32)
        for h in heads:
            cols = slice(128 * h, 128 * (h + 1))
            o = outs[h]
            mu = jnp.mean(o, axis=-1, keepdims=True)
            d = o - mu
            var = jnp.mean(d * d, axis=-1, keepdims=True)
            y = d * lax.rsqrt(var + EPS) * gn_ref[:, cols]
            gate = gate_ref[rows, cols].astype(F32)
            y = y * (gate * (1.0 / (1.0 + jnp.exp(-gate))))
            o_ref[rows, cols] = y.astype(o_ref.dtype)


def _ret_tables():
    T = T_RET
    log_g = np.log1p(-np.exp2(-5.0 - np.arange(RET_HEADS, dtype=np.float64)))
    n = np.arange(T, dtype=np.float64)
    kscale = RET_KEY_DIM ** -0.5
    allowed = (np.arange(T)[None, :] // CHUNK) <= (np.arange(T)[:, None] // CHUNK)
    dmat = np.where(allowed[None],
                    np.exp(log_g[:, None, None] * np.abs(n[:, None] - n[None, :])[None]), 0.0) * kscale
    qdec = np.exp(log_g[:, None] * (n + 1.0)[None]) * kscale
    kdec = np.exp(log_g[:, None] * (T - 1.0 - n)[None])
    cdec = np.exp(log_g * T)
    qdec = np.broadcast_to(qdec[:, :, None], (RET_HEADS, T, 128))
    kdec = np.broadcast_to(kdec[:, :, None], (RET_HEADS, T, 128))
    cdec = np.broadcast_to(cdec[:, None, None], (RET_HEADS, 1, 128))
    return tuple(jnp.asarray(a, F32) for a in (dmat, qdec, kdec, cdec))


def _retention(proj, batch, seq, gn_g):
    T = T_RET
    TS = T * RET_BLOCKS
    nt = seq // TS
    dmat, qdec, kdec, cdec = _ret_tables()
    const = lambda shape: pl.BlockSpec(shape, lambda b, t: (0,) * len(shape))
    return pl.pallas_call(
        _ret_kernel,
        grid=(batch, nt),
        in_specs=[
            pl.BlockSpec((TS, 256), lambda b, t: (b * nt + t, COLB_RQ)),
            pl.BlockSpec((TS, 256), lambda b, t: (b * nt + t, COLB_RK)),
            pl.BlockSpec((TS, 512), lambda b, t: (b * nt + t, COLB_RV)),
            pl.BlockSpec((TS, 512), lambda b, t: (b * nt + t, COLB_RG)),
            const((RET_HEADS, T, T)), const((RET_HEADS, T, 128)),
            const((RET_HEADS, T, 128)), const((RET_HEADS, 1, 128)),
            const((1, RET_WIDTH)),
        ],
        out_specs=pl.BlockSpec((TS, RET_WIDTH), lambda b, t: (b * nt + t, 0)),
        out_shape=jax.ShapeDtypeStruct((batch * seq, RET_WIDTH), BF16),
        scratch_shapes=[pltpu.VMEM((RET_HEADS, 128, 128), F32)],
        compiler_params=pltpu.CompilerParams(
            dimension_semantics=("arbitrary", "arbitrary"), vmem_limit_bytes=VMEM_LIMIT),
        name="retention",
    )(proj, proj, proj, proj, dmat, qdec, kdec, cdec, gn_g)


def _ffn_kernel(tiles_per_seq, x_ref, od_ref, or_ref, wo_ref, g2_ref, wup_ref,
                cw_ref, cb_ref, wd_ref, gf_ref, o_ref, xn_scr, h_scr, halo_scr):
    TM = TM_FFN

    @pl.when(pl.program_id(0) % tiles_per_seq == 0)
    def _():
        halo_scr[...] = jnp.zeros(halo_scr.shape, F32)

    x1 = (x_ref[...]
          + jnp.dot(od_ref[...], wo_ref[0:DIFF_WIDTH, :], preferred_element_type=F32)
          + jnp.dot(or_ref[...], wo_ref[DIFF_WIDTH:, :], preferred_element_type=F32))
    o_ref[...] = x1
    xn_scr[...] = _rms(x1, g2_ref[...]).astype(BF16)

    row = lax.broadcasted_iota(jnp.int32, (TM, TF), 0)

    for c in range(NF):
        xn = xn_scr[...]
        cols = slice(c * TF, (c + 1) * TF)
        gate_cols = slice(D_FF + c * TF, D_FF + (c + 1) * TF)
        a = jnp.dot(xn, wup_ref[:, cols], preferred_element_type=F32)
        b = jnp.dot(xn, wup_ref[:, gate_cols], preferred_element_type=F32)
        halo = halo_scr[c]
        halo_scr[c] = a[TM - 8:TM, :]
        a1 = jnp.where(row == 0, halo[7:8, :], pltpu.roll(a, 1, 0))
        a2 = pltpu.roll(a, 2, 0)
        a2 = jnp.where(row == 0, halo[6:7, :], jnp.where(row == 1, halo[7:8, :], a2))
        u = (cw_ref[0:1, cols] * a2 + cw_ref[1:2, cols] * a1 + cw_ref[2:3, cols] * a
             + cb_ref[:, cols])
        gelu = 0.5 * u * (1.0 + jnp.tanh(0.7978845608028654 * (u + 0.044715 * (u * u * u))))
        h_scr[:, cols] = (gelu * b).astype(BF16)

    y = o_ref[...] + jnp.dot(h_scr[...], wd_ref[...], preferred_element_type=F32)
    o_ref[...] = _rms(y, gf_ref[...])


def _ffn(x2, o_diff, o_ret, wo, g2, wup, cw, cb, wd, gf, seq):
    n = x2.shape[0]
    TM = TM_FFN
    const = lambda shape: pl.BlockSpec(shape, lambda i: (0,) * len(shape),
                                       pipeline_mode=pl.Buffered(1))
    return pl.pallas_call(
        functools.partial(_ffn_kernel, seq // TM),
        grid=(n // TM,),
        in_specs=[
            pl.BlockSpec((TM, D_MODEL), lambda i: (i, 0)),
            pl.BlockSpec((TM, DIFF_WIDTH), lambda i: (i, 0)),
            pl.BlockSpec((TM, RET_WIDTH), lambda i: (i, 0)),
            const((D_MODEL, D_MODEL)), const((1, D_MODEL)),
            const((D_MODEL, 2 * D_FF)),
            const((3, D_FF)), const((1, D_FF)),
            const((D_FF, D_MODEL)), const((1, D_MODEL)),
        ],
        out_specs=pl.BlockSpec((TM, D_MODEL), lambda i: (i, 0)),
        out_shape=jax.ShapeDtypeStruct((n, D_MODEL), F32),
        scratch_shapes=[
            pltpu.VMEM((TM, D_MODEL), BF16),
            pltpu.VMEM((TM, D_FF), BF16),
            pltpu.VMEM((NF, 8, TF), F32),
        ],
        compiler_params=pltpu.CompilerParams(
            dimension_semantics=("arbitrary",), vmem_limit_bytes=VMEM_LIMIT),
        name="outproj_ffn",
    )(x2, o_diff, o_ret, wo, g2, wup, cw, cb, wd, gf)


def kernel(x, norm_mix_g, w_in, lambda_q1, lambda_k1, lambda_q2, lambda_k2, diff_subln_g,
           ret_gn_g, w_out, norm_ffn_g, w_up, conv_w, conv_b, w_down, final_norm_g):
    batch, seq, _ = x.shape
    x2 = x.reshape(batch * seq, D_MODEL)

    proj = _inproj(x2, norm_mix_g, w_in[0].astype(BF16))
    o_diff = _diff_attention(proj, batch, seq, lambda_q1, lambda_k1, lambda_q2, lambda_k2,
                             diff_subln_g)
    o_ret = _retention(proj, batch, seq, ret_gn_g)

    y = _ffn(x2, o_diff, o_ret, w_out[0].astype(BF16), norm_ffn_g, w_up[0].astype(BF16),
             conv_w[0], conv_b[0].reshape(1, D_FF), w_down[0].astype(BF16),
             final_norm_g.reshape(1, D_MODEL), seq)
    return y.reshape(batch, seq, D_MODEL)
```

```python
import functools
import math

import jax
import jax.numpy as jnp
import numpy as np
from jax import lax
from jax.experimental import pallas as pl
from jax.experimental.pallas import tpu as pltpu

D_MODEL = 1024
CHUNK = 64
DIFF_HEADS = 4
DIFF_HEAD_DIM = 64
DIFF_V_DIM = 128
DIFF_WIDTH = 512
RET_HEADS = 4
RET_KEY_DIM = 64
RET_V_DIM = 128
RET_WIDTH = 512
D_FF = 2816
IN_WIDTH = 3072
EPS = 1e-6
LAM_INIT = 0.8 - 0.6 * math.exp(-0.3 * 0)
LOG2E = math.log2(math.e)
MASKED = -1e30

COL_DQ, COL_DK, COL_DV = 0, 4, 8
COLB_RQ, COLB_RK = 6, 7
COLB_RV, COLB_RG = 4, 5

TM_IN = 1024
TQ_ATT = 512
TK_ATT = 512
V_ROWS = DIFF_V_DIM + 16
EXP2_UNDERFLOW = 150.0
BOUND_MARGIN = 2.0
NORM_SLACK = 1.001
KNORM_SLACK = 1.002
OWN_SLACK = 2.0 ** -7
T_RET = 256
RET_BLOCKS = 8
TM_FFN = 1024
TF = 256
NF = D_FF // TF
VMEM_LIMIT = 56 * 1024 * 1024

BF16 = jnp.bfloat16
F32 = jnp.float32


def _rms(x, g):
    ms = jnp.mean(x * x, axis=-1, keepdims=True)
    return x * lax.rsqrt(ms + EPS) * g


def _inproj_kernel(x_ref, g_ref, w_ref, o_ref):
    h = _rms(x_ref[...], g_ref[...]).astype(BF16)
    o_ref[...] = jnp.dot(h, w_ref[...], preferred_element_type=F32).astype(o_ref.dtype)


def _inproj(x2, g, w_bf16):
    n = x2.shape[0]
    return pl.pallas_call(
        _inproj_kernel,
        grid=(n // TM_IN,),
        in_specs=[
            pl.BlockSpec((TM_IN, D_MODEL), lambda i: (i, 0)),
            pl.BlockSpec((1, D_MODEL), lambda i: (0, 0)),
            pl.BlockSpec((D_MODEL, IN_WIDTH), lambda i: (0, 0), pipeline_mode=pl.Buffered(1)),
        ],
        out_specs=pl.BlockSpec((TM_IN, IN_WIDTH), lambda i: (i, 0)),
        out_shape=jax.ShapeDtypeStruct((n, IN_WIDTH), BF16),
        compiler_params=pltpu.CompilerParams(
            dimension_semantics=("arbitrary",), vmem_limit_bytes=VMEM_LIMIT),
        name="inproj",
    )(x2, g, w_bf16)


def _attn_kernel(q_ref, k_ref, v_ref, postab_ref, ctab_ref, cpos_ref, corr_ref,
                 lq1_ref, lk1_ref, lq2_ref, lk2_ref, g_ref, o_ref,
                 kaug_scr, vt_scr, qt_scr, first_scr, nxt_scr, s_scr, m_scr, acc_scr):
    TQ, TK = TQ_ATT, TK_ATT
    DV, DH = DIFF_V_DIM, DIFF_HEAD_DIM
    seq = k_ref.shape[0]
    n_tiles = seq // TK
    tile_lane = lax.broadcasted_iota(jnp.int32, (1, 128), 1)
    lane = lax.broadcasted_iota(jnp.int32, (TK, 128), 1)
    ctab = ctab_ref[0]
    slope_end = cpos_ref[0, 0:1, :]
    slope = cpos_ref[0, 1:2, :]

    ones_row = lax.broadcasted_iota(jnp.int32, (128, 128), 0)
    ones_col = lax.broadcasted_iota(jnp.int32, (128, 128), 1)
    half_ones = jnp.where((ones_row < DH) == (ones_col < DH), 1.0, 0.0).astype(BF16)
    prefix_max = [jnp.zeros((1, 128), F32)] * 2
    running = [jnp.zeros((1, 1), F32)] * 2
    firsts = jnp.zeros((1, 128), jnp.int32)
    for c in range(n_tiles):
        rows = slice(c * TK, (c + 1) * TK)
        kc = k_ref[rows, :]
        pt = postab_ref[rows, :]
        kaug_scr[0, rows, :] = jnp.where(lane < DH, kc, pt)
        kaug_scr[1, rows, :] = jnp.where(lane >= DH, kc, pt)
        vt_scr[c, 0:DV, :] = v_ref[rows, :].T
        vt_scr[c, DV:, :] = jnp.ones((V_ROWS - DV, TK), BF16)
        kf = kc.astype(F32)
        n2 = jnp.max(jnp.dot((kf * kf).astype(BF16), half_ones, preferred_element_type=F32),
                     axis=0, keepdims=True)
        q = q_ref[rows, :].astype(F32) * (DH ** -0.5 * LOG2E)
        own = jnp.min(jnp.dot((q * kf).astype(BF16), half_ones, preferred_element_type=F32),
                      axis=0, keepdims=True)
        skip = tile_lane < c
        for half in range(2):
            in_half = (lane >= DH) if half else (lane < DH)
            in_half_row = (tile_lane >= DH) if half else (tile_lane < DH)
            k_norm = jnp.sqrt(jnp.max(jnp.where(in_half_row, n2, 0.0), axis=1,
                                      keepdims=True)) * KNORM_SLACK
            qb = jnp.where(in_half, q, ctab).astype(BF16).T
            qt_scr[c, half] = qb
            qf = qb[half * DH:(half + 1) * DH, :].astype(F32)
            q_norm = jnp.sqrt(jnp.max(jnp.sum(qf * qf, axis=0, keepdims=True),
                                      axis=1, keepdims=True)) * NORM_SLACK
            own_min = jnp.min(jnp.where(in_half_row, own, jnp.inf), axis=1, keepdims=True)
            upper = q_norm * prefix_max[half] + slope_end
            lower = slope * float(c * TQ) + own_min - OWN_SLACK * (q_norm * k_norm)
            skip = jnp.logical_and(skip, upper + BOUND_MARGIN < lower - EXP2_UNDERFLOW)
            running[half] = jnp.maximum(running[half], k_norm)
            prefix_max[half] = jnp.where(tile_lane == c, running[half], prefix_max[half])
        first_c = jnp.sum(skip.astype(jnp.int32), axis=1, keepdims=True)
        firsts = jnp.where(tile_lane == c, first_c, firsts)
    first_scr[...] = firsts
    acc_scr[...] = jnp.ones(acc_scr.shape, F32)

    lam = (jnp.exp(jnp.sum(lq1_ref[...] * lk1_ref[...], axis=1, keepdims=True))
           - jnp.exp(jnp.sum(lq2_ref[...] * lk2_ref[...], axis=1, keepdims=True))
           + LAM_INIT)

    def scores(i, j, slot, halves=(0, 1)):
        start = pl.multiple_of(j * TK, TK)
        for half in halves:
            s_scr[slot, half] = jnp.dot(kaug_scr[half, pl.ds(start, TK), :], qt_scr[i, half],
                                        preferred_element_type=F32)

    def consume(j, slot, corr, halves=(0, 1)):
        vt = vt_scr[j]
        for half in halves:
            s = s_scr[slot, half]
            if corr is not None:
                s = s + corr
            m_prev = m_scr[half]
            m_new = jnp.maximum(m_prev, jnp.max(s, axis=0, keepdims=True))
            alpha = jnp.exp2(m_prev - m_new)
            p = jnp.exp2(s - m_new).astype(BF16)
            acc_scr[half] = alpha * acc_scr[half] + jnp.dot(
                vt, p, preferred_element_type=F32)
            m_scr[half] = m_new

    def finish(i):
        inv_l0 = 1.0 / acc_scr[0, DV:DV + 1, :]
        inv_l1 = 1.0 / acc_scr[1, DV:DV + 1, :]
        o_t = acc_scr[0, 0:DV, :] * inv_l0 - lam * (acc_scr[1, 0:DV, :] * inv_l1)
        ms = jnp.mean(o_t * o_t, axis=0, keepdims=True)
        y = (o_t * lax.rsqrt(ms + EPS)).T
        row0 = pl.multiple_of(i * TQ, TQ)
        o_ref[pl.ds(row0, TQ), :] = (y * g_ref[...] * (1.0 - LAM_INIT)).astype(o_ref.dtype)

    n_q = seq // TQ

    def q_tile(i, first):
        nxt = jnp.minimum(i + 1, n_q - 1)
        n_past = i - first
        n_rem = n_past - 1

        def start():
            nxt_scr[0] = jnp.sum(jnp.where(tile_lane == nxt, first_scr[...], 0))
            finish(jnp.maximum(i - 1, 0))
            m_scr[...] = jnp.full(m_scr.shape, MASKED, F32)
            acc_scr[...] = jnp.zeros(acc_scr.shape, F32)

        @pl.when(n_past == 0)
        def _():
            start()
            for half in range(2):
                consume(i, 2, corr_ref[0], (half,))
                scores(nxt, nxt_scr[0], 2, (half,))

        @pl.when(n_past > 0)
        def _():
            scores(i, first + 1, 0)
            start()
            consume(first, 2, None)

        def tile_group(j, n):
            scores(i, j + 1, 1)
            for u in range(n - 1):
                for half in range(2):
                    consume(j + u, u % 2, None, (half,))
                    scores(i, j + u + 2, u % 2, (half,))
            consume(j + n - 1, (n - 1) % 2, None)

        def group_loop(base, trips, n):
            def body(t, c):
                tile_group(base + n * t, n)
                return c
            lax.fori_loop(0, trips, body, 0)

        first_nxt = nxt_scr[0]
        n_loop = jnp.maximum(n_rem, 0)
        group_loop(first + 1, n_loop // 4, 4)
        group_loop(first + 1 + 4 * (n_loop // 4), (n_loop % 4) // 2, 2)

        @pl.when(jnp.logical_and(n_past > 0, n_rem % 2 == 0))
        def _():
            scores(nxt, first_nxt, 2)
            consume(i, 0, corr_ref[0])

        @pl.when(jnp.logical_and(n_past > 0, n_rem % 2 == 1))
        def _():
            scores(i, i, 1)
            consume(i - 1, 0, None)
            scores(nxt, first_nxt, 2)
            consume(i, 1, corr_ref[0])

        return first_nxt

    scores(0, 0, 2)
    lax.fori_loop(0, n_q, q_tile, jnp.int32(0))
    finish(n_q - 1)


def _bf16_pieces(x, n):
    x = np.asarray(x, np.float32)
    pieces = []
    for _ in range(n):
        p = x.astype(BF16).astype(np.float32)
        pieces.append(p)
        x = x - p
    return pieces


def _attn_tables(seq):
    TQ, TK = TQ_ATT, TK_ATT
    slopes = (np.exp2(-8.0 * np.arange(1, DIFF_HEADS + 1, dtype=np.float64) / DIFF_HEADS)
              * LOG2E).astype(np.float32)
    c1, c2, c3 = _bf16_pieces(slopes, 3)
    ctab = np.zeros((DIFF_HEADS, 1, 128), np.float32)
    pos = np.arange(seq)
    p_hi, p_lo = (pos // 64) * 64, pos % 64
    postab = np.zeros((seq, 128), np.float32)
    for base in (0, DIFF_HEAD_DIM):
        for n, (cp, pp) in enumerate(((c1, p_hi), (c2, p_hi), (c3, p_hi),
                                      (c1, p_lo), (c2, p_lo), (c3, p_lo))):
            ctab[:, 0, base + n] = cp
            postab[:, base + n] = pp
    c = np.arange(TK)[:, None]
    r = np.arange(TQ)[None, :]
    allowed = (c // CHUNK) <= (r // CHUNK)
    ahead = np.where(c > r, 2.0 * (r - c), 0.0)
    corr = np.where(allowed[None], slopes.astype(np.float64)[:, None, None] * ahead[None], MASKED)
    cpos = np.zeros((DIFF_HEADS, 2, 128), np.float64)
    cpos[:, 0, :] = slopes.astype(np.float64)[:, None] * ((np.arange(128) + 1) * TK - 1)[None, :]
    cpos[:, 1, :] = slopes.astype(np.float64)[:, None]
    return (jnp.asarray(postab, BF16), jnp.asarray(ctab, F32), jnp.asarray(cpos, F32),
            jnp.asarray(corr, F32))


def _diff_attention(proj, batch, seq, lq1, lk1, lq2, lk2, subln_g):
    TQ, TK = TQ_ATT, TK_ATT
    assert TQ == TK and seq % TK == 0
    assert seq // TK <= 128
    postab, ctab, cpos, corr = _attn_tables(seq)
    small = lambda shape: pl.BlockSpec(shape, lambda b, h: (0,) * len(shape))
    return pl.pallas_call(
        _attn_kernel,
        grid=(batch, DIFF_HEADS),
        in_specs=[
            pl.BlockSpec((seq, 128), lambda b, h: (b, COL_DQ + h)),
            pl.BlockSpec((seq, 128), lambda b, h: (b, COL_DK + h)),
            pl.BlockSpec((seq, 128), lambda b, h: (b, COL_DV + h)),
            pl.BlockSpec((seq, 128), lambda b, h: (0, 0), pipeline_mode=pl.Buffered(1)),
            pl.BlockSpec((1, 1, 128), lambda b, h: (h, 0, 0)),
            pl.BlockSpec((1, 2, 128), lambda b, h: (h, 0, 0)),
            pl.BlockSpec((1, TK, TQ), lambda b, h: (h, 0, 0)),
            small((1, DIFF_HEAD_DIM)), small((1, DIFF_HEAD_DIM)),
            small((1, DIFF_HEAD_DIM)), small((1, DIFF_HEAD_DIM)),
            small((1, DIFF_V_DIM)),
        ],
        out_specs=pl.BlockSpec((seq, DIFF_V_DIM), lambda b, h: (b, h)),
        out_shape=jax.ShapeDtypeStruct((batch * seq, DIFF_WIDTH), BF16),
        scratch_shapes=[
            pltpu.VMEM((2, seq, 128), BF16),
            pltpu.VMEM((seq // TK, V_ROWS, TK), BF16),
            pltpu.VMEM((seq // TQ, 2, 128, TQ), BF16),
            pltpu.VMEM((1, 128), jnp.int32),
            pltpu.SMEM((1,), jnp.int32),
            pltpu.VMEM((3, 2, TK, TQ), F32),
            pltpu.VMEM((2, 1, TQ), F32),
            pltpu.VMEM((2, V_ROWS, TQ), F32),
        ],
        compiler_params=pltpu.CompilerParams(
            dimension_semantics=("arbitrary", "arbitrary"), vmem_limit_bytes=VMEM_LIMIT),
        name="diff_attn",
    )(proj, proj, proj, postab, ctab, cpos, corr, lq1, lk1, lq2, lk2, subln_g)


def _ret_kernel(q_ref, k_ref, v_ref, gate_ref, dmat_ref, qdec_ref, kdec_ref, cdec_ref,
                gn_ref, o_ref, state_scr):
    T = T_RET

    @pl.when(pl.program_id(1) == 0)
    def _():
        state_scr[...] = jnp.zeros(state_scr.shape, F32)

    lane = lax.broadcasted_iota(jnp.int32, (T, 128), 1)
    heads = range(RET_HEADS)
    for r in range(RET_BLOCKS):
        rows = slice(r * T, (r + 1) * T)
        qp, km, v, s, outs = {}, {}, {}, {}, {}
        for h in heads:
            pair, half = divmod(h, 2)
            qp[h] = q_ref[rows, 128 * pair:128 * (pair + 1)]
            kp = k_ref[rows, 128 * pair:128 * (pair + 1)]
            in_head = (lane >= RET_KEY_DIM) if half else (lane < RET_KEY_DIM)
            km[h] = jnp.where(in_head, kp, jnp.zeros_like(kp))
            v[h] = v_ref[rows, 128 * h:128 * (h + 1)]
            s[h] = lax.dot_general(qp[h], km[h], (((1,), (1,)), ((), ())),
                                   preferred_element_type=F32)
        for h in heads:
            state = state_scr[h]
            cross = jnp.dot(qp[h], state.astype(BF16), preferred_element_type=F32)
            intra = jnp.dot((s[h] * dmat_ref[h]).astype(BF16), v[h],
                            preferred_element_type=F32)
            outs[h] = intra + qdec_ref[h] * cross
            vd = (v[h].astype(F32) * kdec_ref[h]).astype(BF16)
            state_scr[h] = cdec_ref[h] * state + lax.dot_general(
                km[h], vd, (((0,), (0,)), ((), ())), preferred_element_type=F32)
        for h in heads:
            cols = slice(128 * h, 128 * (h + 1))
            o = outs[h]
            mu = jnp.mean(o, axis=-1, keepdims=True)
            d = o - mu
            var = jnp.mean(d * d, axis=-1, keepdims=True)
            y = d * lax.rsqrt(var + EPS) * gn_ref[:, cols]
            gate = gate_ref[rows, cols].astype(F32)
            y = y * (gate * (1.0 / (1.0 + jnp.exp(-gate))))
            o_ref[rows, cols] = y.astype(o_ref.dtype)


def _ret_tables():
    T = T_RET
    log_g = np.log1p(-np.exp2(-5.0 - np.arange(RET_HEADS, dtype=np.float64)))
    n = np.arange(T, dtype=np.float64)
    kscale = RET_KEY_DIM ** -0.5
    allowed = (np.arange(T)[None, :] // CHUNK) <= (np.arange(T)[:, None] // CHUNK)
    dmat = np.where(allowed[None],
                    np.exp(log_g[:, None, None] * np.abs(n[:, None] - n[None, :])[None]), 0.0) * kscale
    qdec = np.exp(log_g[:, None] * (n + 1.0)[None]) * kscale
    kdec = np.exp(log_g[:, None] * (T - 1.0 - n)[None])
    cdec = np.exp(log_g * T)
    qdec = np.broadcast_to(qdec[:, :, None], (RET_HEADS, T, 128))
    kdec = np.broadcast_to(kdec[:, :, None], (RET_HEADS, T, 128))
    cdec = np.broadcast_to(cdec[:, None, None], (RET_HEADS, 1, 128))
    return tuple(jnp.asarray(a, F32) for a in (dmat, qdec, kdec, cdec))


def _retention(proj, batch, seq, gn_g):
    T = T_RET
    TS = T * RET_BLOCKS
    nt = seq // TS
    dmat, qdec, kdec, cdec = _ret_tables()
    const = lambda shape: pl.BlockSpec(shape, lambda b, t: (0,) * len(shape))
    return pl.pallas_call(
        _ret_kernel,
        grid=(batch, nt),
        in_specs=[
            pl.BlockSpec((TS, 256), lambda b, t: (b * nt + t, COLB_RQ)),
            pl.BlockSpec((TS, 256), lambda b, t: (b * nt + t, COLB_RK)),
            pl.BlockSpec((TS, 512), lambda b, t: (b * nt + t, COLB_RV)),
            pl.BlockSpec((TS, 512), lambda b, t: (b * nt + t, COLB_RG)),
            const((RET_HEADS, T, T)), const((RET_HEADS, T, 128)),
            const((RET_HEADS, T, 128)), const((RET_HEADS, 1, 128)),
            const((1, RET_WIDTH)),
        ],
        out_specs=pl.BlockSpec((TS, RET_WIDTH), lambda b, t: (b * nt + t, 0)),
        out_shape=jax.ShapeDtypeStruct((batch * seq, RET_WIDTH), BF16),
        scratch_shapes=[pltpu.VMEM((RET_HEADS, 128, 128), F32)],
        compiler_params=pltpu.CompilerParams(
            dimension_semantics=("arbitrary", "arbitrary"), vmem_limit_bytes=VMEM_LIMIT),
        name="retention",
    )(proj, proj, proj, proj, dmat, qdec, kdec, cdec, gn_g)


def _ffn_kernel(tiles_per_seq, x_ref, od_ref, or_ref, wo_ref, g2_ref, wup_ref,
                cw_ref, cb_ref, wd_ref, gf_ref, o_ref, xn_scr, h_scr, halo_scr):
    TM = TM_FFN

    @pl.when(pl.program_id(0) % tiles_per_seq == 0)
    def _():
        halo_scr[...] = jnp.zeros(halo_scr.shape, F32)

    x1 = (x_ref[...]
          + jnp.dot(od_ref[...], wo_ref[0:DIFF_WIDTH, :], preferred_element_type=F32)
          + jnp.dot(or_ref[...], wo_ref[DIFF_WIDTH:, :], preferred_element_type=F32))
    o_ref[...] = x1
    xn_scr[...] = _rms(x1, g2_ref[...]).astype(BF16)

    row = lax.broadcasted_iota(jnp.int32, (TM, TF), 0)

    for c in range(NF):
        xn = xn_scr[...]
        cols = slice(c * TF, (c + 1) * TF)
        gate_cols = slice(D_FF + c * TF, D_FF + (c + 1) * TF)
        a = jnp.dot(xn, wup_ref[:, cols], preferred_element_type=F32)
        b = jnp.dot(xn, wup_ref[:, gate_cols], preferred_element_type=F32)
        halo = halo_scr[c]
        halo_scr[c] = a[TM - 8:TM, :]
        a1 = jnp.where(row == 0, halo[7:8, :], pltpu.roll(a, 1, 0))
        a2 = pltpu.roll(a, 2, 0)
        a2 = jnp.where(row == 0, halo[6:7, :], jnp.where(row == 1, halo[7:8, :], a2))
        u = (cw_ref[0:1, cols] * a2 + cw_ref[1:2, cols] * a1 + cw_ref[2:3, cols] * a
             + cb_ref[:, cols])
        gelu = 0.5 * u * (1.0 + jnp.tanh(0.7978845608028654 * (u + 0.044715 * (u * u * u))))
        h_scr[:, cols] = (gelu * b).astype(BF16)

    y = o_ref[...] + jnp.dot(h_scr[...], wd_ref[...], preferred_element_type=F32)
    o_ref[...] = _rms(y, gf_ref[...])


def _ffn(x2, o_diff, o_ret, wo, g2, wup, cw, cb, wd, gf, seq):
    n = x2.shape[0]
    TM = TM_FFN
    const = lambda shape: pl.BlockSpec(shape, lambda i: (0,) * len(shape),
                                       pipeline_mode=pl.Buffered(1))
    return pl.pallas_call(
        functools.partial(_ffn_kernel, seq // TM),
        grid=(n // TM,),
        in_specs=[
            pl.BlockSpec((TM, D_MODEL), lambda i: (i, 0)),
            pl.BlockSpec((TM, DIFF_WIDTH), lambda i: (i, 0)),
            pl.BlockSpec((TM, RET_WIDTH), lambda i: (i, 0)),
            const((D_MODEL, D_MODEL)), const((1, D_MODEL)),
            const((D_MODEL, 2 * D_FF)),
            const((3, D_FF)), const((1, D_FF)),
            const((D_FF, D_MODEL)), const((1, D_MODEL)),
        ],
        out_specs=pl.BlockSpec((TM, D_MODEL), lambda i: (i, 0)),
        out_shape=jax.ShapeDtypeStruct((n, D_MODEL), F32),
        scratch_shapes=[
            pltpu.VMEM((TM, D_MODEL), BF16),
            pltpu.VMEM((TM, D_FF), BF16),
            pltpu.VMEM((NF, 8, TF), F32),
        ],
        compiler_params=pltpu.CompilerParams(
            dimension_semantics=("arbitrary",), vmem_limit_bytes=VMEM_LIMIT),
        name="outproj_ffn",
    )(x2, o_diff, o_ret, wo, g2, wup, cw, cb, wd, gf)


def kernel(x, norm_mix_g, w_in, lambda_q1, lambda_k1, lambda_q2, lambda_k2, diff_subln_g,
           ret_gn_g, w_out, norm_ffn_g, w_up, conv_w, conv_b, w_down, final_norm_g):
    batch, seq, _ = x.shape
    x2 = x.reshape(batch * seq, D_MODEL)

    proj = _inproj(x2, norm_mix_g, w_in[0].astype(BF16))
    o_diff = _diff_attention(proj, batch, seq, lambda_q1, lambda_k1, lambda_q2, lambda_k2,
                             diff_subln_g)
    o_ret = _retention(proj, batch, seq, ret_gn_g)

    y = _ffn(x2, o_diff, o_ret, w_out[0].astype(BF16), norm_ffn_g, w_up[0].astype(BF16),
             conv_w[0], conv_b[0].reshape(1, D_FF), w_down[0].astype(BF16),
             final_norm_g.reshape(1, D_MODEL), seq)
    return y.reshape(batch, seq, D_MODEL)
```

```python
import functools
import math

import jax
import jax.numpy as jnp
import numpy as np
from jax import lax
from jax.experimental import pallas as pl
from jax.experimental.pallas import tpu as pltpu

D_MODEL = 1024
CHUNK = 64
DIFF_HEADS = 4
DIFF_HEAD_DIM = 64
DIFF_V_DIM = 128
DIFF_WIDTH = 512
RET_HEADS = 4
RET_KEY_DIM = 64
RET_V_DIM = 128
RET_WIDTH = 512
D_FF = 2816
IN_WIDTH = 3072
EPS = 1e-6
LAM_INIT = 0.8 - 0.6 * math.exp(-0.3 * 0)
LOG2E = math.log2(math.e)
MASKED = -1e30

COL_DQ, COL_DK, COL_DV = 0, 4, 8
COLB_RQ, COLB_RK = 6, 7
COLB_RV, COLB_RG = 4, 5

TM_IN = 1024
TQ_ATT = 512
TK_ATT = 512
V_ROWS = DIFF_V_DIM + 16
EXP2_UNDERFLOW = 150.0
BOUND_MARGIN = 2.0
NORM_SLACK = 1.001
KNORM_SLACK = 1.002
OWN_SLACK = 2.0 ** -7
T_RET = 256
RET_BLOCKS = 8
TM_FFN = 1024
TF = 256
NF = D_FF // TF
VMEM_LIMIT = 56 * 1024 * 1024

BF16 = jnp.bfloat16
F32 = jnp.float32


def _rms(x, g):
    ms = jnp.mean(x * x, axis=-1, keepdims=True)
    return x * lax.rsqrt(ms + EPS) * g


def _inproj_kernel(x_ref, g_ref, w_ref, o_ref):
    h = _rms(x_ref[...], g_ref[...]).astype(BF16)
    o_ref[...] = jnp.dot(h, w_ref[...], preferred_element_type=F32).astype(o_ref.dtype)


def _inproj(x2, g, w_bf16):
    n = x2.shape[0]
    return pl.pallas_call(
        _inproj_kernel,
        grid=(n // TM_IN,),
        in_specs=[
            pl.BlockSpec((TM_IN, D_MODEL), lambda i: (i, 0)),
            pl.BlockSpec((1, D_MODEL), lambda i: (0, 0)),
            pl.BlockSpec((D_MODEL, IN_WIDTH), lambda i: (0, 0), pipeline_mode=pl.Buffered(1)),
        ],
        out_specs=pl.BlockSpec((TM_IN, IN_WIDTH), lambda i: (i, 0)),
        out_shape=jax.ShapeDtypeStruct((n, IN_WIDTH), BF16),
        compiler_params=pltpu.CompilerParams(
            dimension_semantics=("arbitrary",), vmem_limit_bytes=VMEM_LIMIT),
        name="inproj",
    )(x2, g, w_bf16)


def _attn_kernel(q_ref, k_ref, v_ref, postab_ref, ctab_ref, cpos_ref, corr_ref,
                 lq1_ref, lk1_ref, lq2_ref, lk2_ref, g_ref, o_ref,
                 kaug_scr, vt_scr, qt_scr, first_scr, nxt_scr, s_scr, m_scr, acc_scr):
    TQ, TK = TQ_ATT, TK_ATT
    DV, DH = DIFF_V_DIM, DIFF_HEAD_DIM
    seq = k_ref.shape[0]
    n_tiles = seq // TK
    tile_lane = lax.broadcasted_iota(jnp.int32, (1, 128), 1)
    lane = lax.broadcasted_iota(jnp.int32, (TK, 128), 1)
    ctab = ctab_ref[0]
    slope_end = cpos_ref[0, 0:1, :]
    slope = cpos_ref[0, 1:2, :]

    ones_row = lax.broadcasted_iota(jnp.int32, (128, 128), 0)
    ones_col = lax.broadcasted_iota(jnp.int32, (128, 128), 1)
    half_ones = jnp.where((ones_row < DH) == (ones_col < DH), 1.0, 0.0).astype(BF16)
    prefix_max = [jnp.zeros((1, 128), F32)] * 2
    running = [jnp.zeros((1, 1), F32)] * 2
    firsts = jnp.zeros((1, 128), jnp.int32)
    for c in range(n_tiles):
        rows = slice(c * TK, (c + 1) * TK)
        kc = k_ref[rows, :]
        pt = postab_ref[rows, :]
        kaug_scr[0, rows, :] = jnp.where(lane < DH, kc, pt)
        kaug_scr[1, rows, :] = jnp.where(lane >= DH, kc, pt)
        vt_scr[c, 0:DV, :] = v_ref[rows, :].T
        vt_scr[c, DV:, :] = jnp.ones((V_ROWS - DV, TK), BF16)
        kf = kc.astype(F32)
        n2 = jnp.max(jnp.dot((kf * kf).astype(BF16), half_ones, preferred_element_type=F32),
                     axis=0, keepdims=True)
        q = q_ref[rows, :].astype(F32) * (DH ** -0.5 * LOG2E)
        own = jnp.min(jnp.dot((q * kf).astype(BF16), half_ones, preferred_element_type=F32),
                      axis=0, keepdims=True)
        skip = tile_lane < c
        for half in range(2):
            in_half = (lane >= DH) if half else (lane < DH)
            in_half_row = (tile_lane >= DH) if half else (tile_lane < DH)
            k_norm = jnp.sqrt(jnp.max(jnp.where(in_half_row, n2, 0.0), axis=1,
                                      keepdims=True)) * KNORM_SLACK
            qb = jnp.where(in_half, q, ctab).astype(BF16).T
            qt_scr[c, half] = qb
            qf = qb[half * DH:(half + 1) * DH, :].astype(F32)
            q_norm = jnp.sqrt(jnp.max(jnp.sum(qf * qf, axis=0, keepdims=True),
                                      axis=1, keepdims=True)) * NORM_SLACK
            own_min = jnp.min(jnp.where(in_half_row, own, jnp.inf), axis=1, keepdims=True)
            upper = q_norm * prefix_max[half] + slope_end
            lower = slope * float(c * TQ) + own_min - OWN_SLACK * (q_norm * k_norm)
            skip = jnp.logical_and(skip, upper + BOUND_MARGIN < lower - EXP2_UNDERFLOW)
            running[half] = jnp.maximum(running[half], k_norm)
            prefix_max[half] = jnp.where(tile_lane == c, running[half], prefix_max[half])
        first_c = jnp.sum(skip.astype(jnp.int32), axis=1, keepdims=True)
        firsts = jnp.where(tile_lane == c, first_c, firsts)
    first_scr[...] = firsts
    acc_scr[...] = jnp.ones(acc_scr.shape, F32)

    lam = (jnp.exp(jnp.sum(lq1_ref[...] * lk1_ref[...], axis=1, keepdims=True))
           - jnp.exp(jnp.sum(lq2_ref[...] * lk2_ref[...], axis=1, keepdims=True))
           + LAM_INIT)

    def scores(i, j, slot, halves=(0, 1)):
        start = pl.multiple_of(j * TK, TK)
        for half in halves:
            s_scr[slot, half] = jnp.dot(kaug_scr[half, pl.ds(start, TK), :], qt_scr[i, half],
                                        preferred_element_type=F32)

    def consume(j, slot, corr, halves=(0, 1)):
        vt = vt_scr[j]
        for half in halves:
            s = s_scr[slot, half]
            if corr is not None:
                s = s + corr
            m_prev = m_scr[half]
            m_new = jnp.maximum(m_prev, jnp.max(s, axis=0, keepdims=True))
            alpha = jnp.exp2(m_prev - m_new)
            p = jnp.exp2(s - m_new).astype(BF16)
            acc_scr[half] = alpha * acc_scr[half] + jnp.dot(
                vt, p, preferred_element_type=F32)
            m_scr[half] = m_new

    def finish(i):
        inv_l0 = 1.0 / acc_scr[0, DV:DV + 1, :]
        inv_l1 = 1.0 / acc_scr[1, DV:DV + 1, :]
        o_t = acc_scr[0, 0:DV, :] * inv_l0 - lam * (acc_scr[1, 0:DV, :] * inv_l1)
        ms = jnp.mean(o_t * o_t, axis=0, keepdims=True)
        y = (o_t * lax.rsqrt(ms + EPS)).T
        row0 = pl.multiple_of(i * TQ, TQ)
        o_ref[pl.ds(row0, TQ), :] = (y * g_ref[...] * (1.0 - LAM_INIT)).astype(o_ref.dtype)

    n_q = seq // TQ

    def q_tile(i, first):
        nxt = jnp.minimum(i + 1, n_q - 1)
        n_past = i - first
        n_rem = n_past - 1

        def start():
            nxt_scr[0] = jnp.sum(jnp.where(tile_lane == nxt, first_scr[...], 0))
            finish(jnp.maximum(i - 1, 0))
            m_scr[...] = jnp.full(m_scr.shape, MASKED, F32)
            acc_scr[...] = jnp.zeros(acc_scr.shape, F32)

        @pl.when(n_past == 0)
        def _():
            start()
            for half in range(2):
                consume(i, 2, corr_ref[0], (half,))
                scores(nxt, nxt_scr[0], 2, (half,))

        @pl.when(n_past == 1)
        def _():
            scores(i, i, 0)
            start()
            for half in range(2):
                consume(first, 2, None, (half,))
                scores(nxt, nxt_scr[0], 2, (half,))
            consume(i, 0, corr_ref[0])

        @pl.when(n_past > 1)
        def _():
            scores(i, first + 1, 0)
            start()
            consume(first, 2, None)

        def tile_group(j, n):
            scores(i, j + 1, 1)
            for u in range(n - 1):
                for half in range(2):
                    consume(j + u, u % 2, None, (half,))
                    scores(i, j + u + 2, u % 2, (half,))
            consume(j + n - 1, (n - 1) % 2, None)

        def group_loop(base, trips, n):
            def body(t, c):
                tile_group(base + n * t, n)
                return c
            lax.fori_loop(0, trips, body, 0)

        first_nxt = nxt_scr[0]
        n_loop = jnp.maximum(n_rem, 0)
        group_loop(first + 1, n_loop // 4, 4)
        group_loop(first + 1 + 4 * (n_loop // 4), (n_loop % 4) // 2, 2)

        @pl.when(jnp.logical_and(n_past > 1, n_rem % 2 == 0))
        def _():
            scores(nxt, first_nxt, 2)
            consume(i, 0, corr_ref[0])

        @pl.when(jnp.logical_and(n_past > 1, n_rem % 2 == 1))
        def _():
            scores(i, i, 1)
            consume(i - 1, 0, None)
            scores(nxt, first_nxt, 2)
            consume(i, 1, corr_ref[0])

        return first_nxt

    scores(0, 0, 2)
    lax.fori_loop(0, n_q, q_tile, jnp.int32(0))
    finish(n_q - 1)


def _bf16_pieces(x, n):
    x = np.asarray(x, np.float32)
    pieces = []
    for _ in range(n):
        p = x.astype(BF16).astype(np.float32)
        pieces.append(p)
        x = x - p
    return pieces


def _attn_tables(seq):
    TQ, TK = TQ_ATT, TK_ATT
    slopes = (np.exp2(-8.0 * np.arange(1, DIFF_HEADS + 1, dtype=np.float64) / DIFF_HEADS)
              * LOG2E).astype(np.float32)
    c1, c2, c3 = _bf16_pieces(slopes, 3)
    ctab = np.zeros((DIFF_HEADS, 1, 128), np.float32)
    pos = np.arange(seq)
    p_hi, p_lo = (pos // 64) * 64, pos % 64
    postab = np.zeros((seq, 128), np.float32)
    for base in (0, DIFF_HEAD_DIM):
        for n, (cp, pp) in enumerate(((c1, p_hi), (c2, p_hi), (c3, p_hi),
                                      (c1, p_lo), (c2, p_lo), (c3, p_lo))):
            ctab[:, 0, base + n] = cp
            postab[:, base + n] = pp
    c = np.arange(TK)[:, None]
    r = np.arange(TQ)[None, :]
    allowed = (c // CHUNK) <= (r // CHUNK)
    ahead = np.where(c > r, 2.0 * (r - c), 0.0)
    corr = np.where(allowed[None], slopes.astype(np.float64)[:, None, None] * ahead[None], MASKED)
    cpos = np.zeros((DIFF_HEADS, 2, 128), np.float64)
    cpos[:, 0, :] = slopes.astype(np.float64)[:, None] * ((np.arange(128) + 1) * TK - 1)[None, :]
    cpos[:, 1, :] = slopes.astype(np.float64)[:, None]
    return (jnp.asarray(postab, BF16), jnp.asarray(ctab, F32), jnp.asarray(cpos, F32),
            jnp.asarray(corr, F32))


def _diff_attention(proj, batch, seq, lq1, lk1, lq2, lk2, subln_g):
    TQ, TK = TQ_ATT, TK_ATT
    assert TQ == TK and seq % TK == 0
    assert seq // TK <= 128
    postab, ctab, cpos, corr = _attn_tables(seq)
    small = lambda shape: pl.BlockSpec(shape, lambda b, h: (0,) * len(shape))
    return pl.pallas_call(
        _attn_kernel,
        grid=(batch, DIFF_HEADS),
        in_specs=[
            pl.BlockSpec((seq, 128), lambda b, h: (b, COL_DQ + h)),
            pl.BlockSpec((seq, 128), lambda b, h: (b, COL_DK + h)),
            pl.BlockSpec((seq, 128), lambda b, h: (b, COL_DV + h)),
            pl.BlockSpec((seq, 128), lambda b, h: (0, 0), pipeline_mode=pl.Buffered(1)),
            pl.BlockSpec((1, 1, 128), lambda b, h: (h, 0, 0)),
            pl.BlockSpec((1, 2, 128), lambda b, h: (h, 0, 0)),
            pl.BlockSpec((1, TK, TQ), lambda b, h: (h, 0, 0)),
            small((1, DIFF_HEAD_DIM)), small((1, DIFF_HEAD_DIM)),
            small((1, DIFF_HEAD_DIM)), small((1, DIFF_HEAD_DIM)),
            small((1, DIFF_V_DIM)),
        ],
        out_specs=pl.BlockSpec((seq, DIFF_V_DIM), lambda b, h: (b, h)),
        out_shape=jax.ShapeDtypeStruct((batch * seq, DIFF_WIDTH), BF16),
        scratch_shapes=[
            pltpu.VMEM((2, seq, 128), BF16),
            pltpu.VMEM((seq // TK, V_ROWS, TK), BF16),
            pltpu.VMEM((seq // TQ, 2, 128, TQ), BF16),
            pltpu.VMEM((1, 128), jnp.int32),
            pltpu.SMEM((1,), jnp.int32),
            pltpu.VMEM((3, 2, TK, TQ), F32),
            pltpu.VMEM((2, 1, TQ), F32),
            pltpu.VMEM((2, V_ROWS, TQ), F32),
        ],
        compiler_params=pltpu.CompilerParams(
            dimension_semantics=("arbitrary", "arbitrary"), vmem_limit_bytes=VMEM_LIMIT),
        name="diff_attn",
    )(proj, proj, proj, postab, ctab, cpos, corr, lq1, lk1, lq2, lk2, subln_g)


def _ret_kernel(q_ref, k_ref, v_ref, gate_ref, dmat_ref, qdec_ref, kdec_ref, cdec_ref,
                gn_ref, o_ref, state_scr):
    T = T_RET

    @pl.when(pl.program_id(1) == 0)
    def _():
        state_scr[...] = jnp.zeros(state_scr.shape, F32)

    lane = lax.broadcasted_iota(jnp.int32, (T, 128), 1)
    heads = range(RET_HEADS)
    for r in range(RET_BLOCKS):
        rows = slice(r * T, (r + 1) * T)
        qp, km, v, s, outs = {}, {}, {}, {}, {}
        for h in heads:
            pair, half = divmod(h, 2)
            qp[h] = q_ref[rows, 128 * pair:128 * (pair + 1)]
            kp = k_ref[rows, 128 * pair:128 * (pair + 1)]
            in_head = (lane >= RET_KEY_DIM) if half else (lane < RET_KEY_DIM)
            km[h] = jnp.where(in_head, kp, jnp.zeros_like(kp))
            v[h] = v_ref[rows, 128 * h:128 * (h + 1)]
            s[h] = lax.dot_general(qp[h], km[h], (((1,), (1,)), ((), ())),
                                   preferred_element_type=F32)
        for h in heads:
            state = state_scr[h]
            cross = jnp.dot(qp[h], state.astype(BF16), preferred_element_type=F32)
            intra = jnp.dot((s[h] * dmat_ref[h]).astype(BF16), v[h],
                            preferred_element_type=F32)
            outs[h] = intra + qdec_ref[h] * cross
            vd = (v[h].astype(F32) * kdec_ref[h]).astype(BF16)
            state_scr[h] = cdec_ref[h] * state + lax.dot_general(
                km[h], vd, (((0,), (0,)), ((), ())), preferred_element_type=F32)
        for h in heads:
            cols = slice(128 * h, 128 * (h + 1))
            o = outs[h]
            mu = jnp.mean(o, axis=-1, keepdims=True)
            d = o - mu
            var = jnp.mean(d * d, axis=-1, keepdims=True)
            y = d * lax.rsqrt(var + EPS) * gn_ref[:, cols]
            gate = gate_ref[rows, cols].astype(F32)
            y = y * (gate * (1.0 / (1.0 + jnp.exp(-gate))))
            o_ref[rows, cols] = y.astype(o_ref.dtype)


def _ret_tables():
    T = T_RET
    log_g = np.log1p(-np.exp2(-5.0 - np.arange(RET_HEADS, dtype=np.float64)))
    n = np.arange(T, dtype=np.float64)
    kscale = RET_KEY_DIM ** -0.5
    allowed = (np.arange(T)[None, :] // CHUNK) <= (np.arange(T)[:, None] // CHUNK)
    dmat = np.where(allowed[None],
                    np.exp(log_g[:, None, None] * np.abs(n[:, None] - n[None, :])[None]), 0.0) * kscale
    qdec = np.exp(log_g[:, None] * (n + 1.0)[None]) * kscale
    kdec = np.exp(log_g[:, None] * (T - 1.0 - n)[None])
    cdec = np.exp(log_g * T)
    qdec = np.broadcast_to(qdec[:, :, None], (RET_HEADS, T, 128))
    kdec = np.broadcast_to(kdec[:, :, None], (RET_HEADS, T, 128))
    cdec = np.broadcast_to(cdec[:, None, None], (RET_HEADS, 1, 128))
    return tuple(jnp.asarray(a, F32) for a in (dmat, qdec, kdec, cdec))


def _retention(proj, batch, seq, gn_g):
    T = T_RET
    TS = T * RET_BLOCKS
    nt = seq // TS
    dmat, qdec, kdec, cdec = _ret_tables()
    const = lambda shape: pl.BlockSpec(shape, lambda b, t: (0,) * len(shape))
    return pl.pallas_call(
        _ret_kernel,
        grid=(batch, nt),
        in_specs=[
            pl.BlockSpec((TS, 256), lambda b, t: (b * nt + t, COLB_RQ)),
            pl.BlockSpec((TS, 256), lambda b, t: (b * nt + t, COLB_RK)),
            pl.BlockSpec((TS, 512), lambda b, t: (b * nt + t, COLB_RV)),
            pl.BlockSpec((TS, 512), lambda b, t: (b * nt + t, COLB_RG)),
            const((RET_HEADS, T, T)), const((RET_HEADS, T, 128)),
            const((RET_HEADS, T, 128)), const((RET_HEADS, 1, 128)),
            const((1, RET_WIDTH)),
        ],
        out_specs=pl.BlockSpec((TS, RET_WIDTH), lambda b, t: (b * nt + t, 0)),
        out_shape=jax.ShapeDtypeStruct((batch * seq, RET_WIDTH), BF16),
        scratch_shapes=[pltpu.VMEM((RET_HEADS, 128, 128), F32)],
        compiler_params=pltpu.CompilerParams(
            dimension_semantics=("arbitrary", "arbitrary"), vmem_limit_bytes=VMEM_LIMIT),
        name="retention",
    )(proj, proj, proj, proj, dmat, qdec, kdec, cdec, gn_g)


def _ffn_kernel(tiles_per_seq, x_ref, od_ref, or_ref, wo_ref, g2_ref, wup_ref,
                cw_ref, cb_ref, wd_ref, gf_ref, o_ref, xn_scr, h_scr, halo_scr):
    TM = TM_FFN

    @pl.when(pl.program_id(0) % tiles_per_seq == 0)
    def _():
        halo_scr[...] = jnp.zeros(halo_scr.shape, F32)

    SUB = TM // 2
    row = lax.broadcasted_iota(jnp.int32, (SUB, TF), 0)
    halos = [halo_scr[c] for c in range(NF)]
    groups = (slice(0, SUB), slice(SUB, TM))

    def out_proj(rows):
        x1 = (x_ref[rows, :]
              + jnp.dot(od_ref[rows, :], wo_ref[0:DIFF_WIDTH, :], preferred_element_type=F32)
              + jnp.dot(or_ref[rows, :], wo_ref[DIFF_WIDTH:, :], preferred_element_type=F32))
        o_ref[rows, :] = x1

    def pre_norm(rows):
        xn_scr[rows, :] = _rms(o_ref[rows, :], g2_ref[...]).astype(BF16)

    def chunk(rows, c):
        xn = xn_scr[rows, :]
        cols = slice(c * TF, (c + 1) * TF)
        gate_cols = slice(D_FF + c * TF, D_FF + (c + 1) * TF)
        a = jnp.dot(xn, wup_ref[:, cols], preferred_element_type=F32)
        b = jnp.dot(xn, wup_ref[:, gate_cols], preferred_element_type=F32)
        halo = halos[c]
        halos[c] = a[SUB - 8:SUB, :]
        a1 = jnp.where(row == 0, halo[7:8, :], pltpu.roll(a, 1, 0))
        a2 = pltpu.roll(a, 2, 0)
        a2 = jnp.where(row == 0, halo[6:7, :], jnp.where(row == 1, halo[7:8, :], a2))
        u = (cw_ref[0:1, cols] * a2 + cw_ref[1:2, cols] * a1 + cw_ref[2:3, cols] * a
             + cb_ref[:, cols])
        gelu = 0.5 * u * (1.0 + jnp.tanh(0.7978845608028654 * (u + 0.044715 * (u * u * u))))
        h_scr[rows, cols] = (gelu * b).astype(BF16)

    def down_proj(rows):
        o_ref[rows, :] = o_ref[rows, :] + jnp.dot(h_scr[rows, :], wd_ref[...],
                                                  preferred_element_type=F32)

    def post_norm(rows):
        o_ref[rows, :] = _rms(o_ref[rows, :], gf_ref[...])

    g0, g1 = groups
    out_proj(g0)
    out_proj(g1)
    pre_norm(g0)
    chunk(g0, 0)
    pre_norm(g1)
    for c in range(1, NF):
        chunk(g0, c)
    down_proj(g0)
    chunk(g1, 0)
    post_norm(g0)
    for c in range(1, NF):
        chunk(g1, c)
    down_proj(g1)
    post_norm(g1)
    for c in range(NF):
        halo_scr[c] = halos[c]


def _ffn(x2, o_diff, o_ret, wo, g2, wup, cw, cb, wd, gf, seq):
    n = x2.shape[0]
    TM = TM_FFN
    const = lambda shape: pl.BlockSpec(shape, lambda i: (0,) * len(shape),
                                       pipeline_mode=pl.Buffered(1))
    return pl.pallas_call(
        functools.partial(_ffn_kernel, seq // TM),
        grid=(n // TM,),
        in_specs=[
            pl.BlockSpec((TM, D_MODEL), lambda i: (i, 0)),
            pl.BlockSpec((TM, DIFF_WIDTH), lambda i: (i, 0)),
            pl.BlockSpec((TM, RET_WIDTH), lambda i: (i, 0)),
            const((D_MODEL, D_MODEL)), const((1, D_MODEL)),
            const((D_MODEL, 2 * D_FF)),
            const((3, D_FF)), const((1, D_FF)),
            const((D_FF, D_MODEL)), const((1, D_MODEL)),
        ],
        out_specs=pl.BlockSpec((TM, D_MODEL), lambda i: (i, 0)),
        out_shape=jax.ShapeDtypeStruct((n, D_MODEL), F32),
        scratch_shapes=[
            pltpu.VMEM((TM, D_MODEL), BF16),
            pltpu.VMEM((TM, D_FF), BF16),
            pltpu.VMEM((NF, 8, TF), F32),
        ],
        compiler_params=pltpu.CompilerParams(
            dimension_semantics=("arbitrary",), vmem_limit_bytes=VMEM_LIMIT),
        name="outproj_ffn",
    )(x2, o_diff, o_ret, wo, g2, wup, cw, cb, wd, gf)


def kernel(x, norm_mix_g, w_in, lambda_q1, lambda_k1, lambda_q2, lambda_k2, diff_subln_g,
           ret_gn_g, w_out, norm_ffn_g, w_up, conv_w, conv_b, w_down, final_norm_g):
    batch, seq, _ = x.shape
    x2 = x.reshape(batch * seq, D_MODEL)

    proj = _inproj(x2, norm_mix_g, w_in[0].astype(BF16))
    o_diff = _diff_attention(proj, batch, seq, lambda_q1, lambda_k1, lambda_q2, lambda_k2,
                             diff_subln_g)
    o_ret = _retention(proj, batch, seq, ret_gn_g)

    y = _ffn(x2, o_diff, o_ret, w_out[0].astype(BF16), norm_ffn_g, w_up[0].astype(BF16),
             conv_w[0], conv_b[0].reshape(1, D_FF), w_down[0].astype(BF16),
             final_norm_g.reshape(1, D_MODEL), seq)
    return y.reshape(batch, seq, D_MODEL)
```

```python
import functools
import math

import jax
import jax.numpy as jnp
import numpy as np
from jax import lax
from jax.experimental import pallas as pl
from jax.experimental.pallas import tpu as pltpu

D_MODEL = 1024
CHUNK = 64
DIFF_HEADS = 4
DIFF_HEAD_DIM = 64
DIFF_V_DIM = 128
DIFF_WIDTH = 512
RET_HEADS = 4
RET_KEY_DIM = 64
RET_V_DIM = 128
RET_WIDTH = 512
D_FF = 2816
IN_WIDTH = 3072
EPS = 1e-6
LAM_INIT = 0.8 - 0.6 * math.exp(-0.3 * 0)
LOG2E = math.log2(math.e)
MASKED = -1e30

COL_DQ, COL_DK, COL_DV = 0, 4, 8
COLB_RQ, COLB_RK = 6, 7
COLB_RV, COLB_RG = 4, 5

TM_IN = 1024
TQ_ATT = 512
TK_ATT = 512
V_ROWS = DIFF_V_DIM + 16
EXP2_UNDERFLOW = 150.0
BOUND_MARGIN = 2.0
NORM_SLACK = 1.001
SHORT_ROW_TILES = 4
KNORM_SLACK = 1.002
OWN_SLACK = 2.0 ** -7
T_RET = 256
RET_BLOCKS = 8
TM_FFN = 1024
TF = 256
NF = D_FF // TF
VMEM_LIMIT = 56 * 1024 * 1024

BF16 = jnp.bfloat16
F32 = jnp.float32


def _rms(x, g):
    ms = jnp.mean(x * x, axis=-1, keepdims=True)
    return x * lax.rsqrt(ms + EPS) * g


def _inproj_kernel(x_ref, g_ref, w_ref, o_ref):
    h = _rms(x_ref[...], g_ref[...]).astype(BF16)
    o_ref[...] = jnp.dot(h, w_ref[...], preferred_element_type=F32).astype(o_ref.dtype)


def _inproj(x2, g, w_bf16):
    n = x2.shape[0]
    return pl.pallas_call(
        _inproj_kernel,
        grid=(n // TM_IN,),
        in_specs=[
            pl.BlockSpec((TM_IN, D_MODEL), lambda i: (i, 0)),
            pl.BlockSpec((1, D_MODEL), lambda i: (0, 0)),
            pl.BlockSpec((D_MODEL, IN_WIDTH), lambda i: (0, 0), pipeline_mode=pl.Buffered(1)),
        ],
        out_specs=pl.BlockSpec((TM_IN, IN_WIDTH), lambda i: (i, 0)),
        out_shape=jax.ShapeDtypeStruct((n, IN_WIDTH), BF16),
        compiler_params=pltpu.CompilerParams(
            dimension_semantics=("arbitrary",), vmem_limit_bytes=VMEM_LIMIT),
        name="inproj",
    )(x2, g, w_bf16)


def _attn_kernel(q_ref, k_ref, v_ref, postab_ref, ctab_ref, cpos_ref, corr_ref,
                 lq1_ref, lk1_ref, lq2_ref, lk2_ref, g_ref, o_ref,
                 kaug_scr, vt_scr, qt_scr, first_scr, nxt_scr, s_scr, m_scr, acc_scr):
    TQ, TK = TQ_ATT, TK_ATT
    DV, DH = DIFF_V_DIM, DIFF_HEAD_DIM
    seq = k_ref.shape[0]
    n_tiles = seq // TK
    tile_lane = lax.broadcasted_iota(jnp.int32, (1, 128), 1)
    lane = lax.broadcasted_iota(jnp.int32, (TK, 128), 1)
    ctab = ctab_ref[0]
    slope_end = cpos_ref[0, 0:1, :]
    slope = cpos_ref[0, 1:2, :]

    ones_row = lax.broadcasted_iota(jnp.int32, (128, 128), 0)
    ones_col = lax.broadcasted_iota(jnp.int32, (128, 128), 1)
    half_ones = jnp.where((ones_row < DH) == (ones_col < DH), 1.0, 0.0).astype(BF16)
    prefix_max = [jnp.zeros((1, 128), F32)] * 2
    running = [jnp.zeros((1, 1), F32)] * 2
    firsts = jnp.zeros((1, 128), jnp.int32)
    for c in range(n_tiles):
        rows = slice(c * TK, (c + 1) * TK)
        kc = k_ref[rows, :]
        pt = postab_ref[rows, :]
        kaug_scr[0, rows, :] = jnp.where(lane < DH, kc, pt)
        kaug_scr[1, rows, :] = jnp.where(lane >= DH, kc, pt)
        vt_scr[c, 0:DV, :] = v_ref[rows, :].T
        vt_scr[c, DV:, :] = jnp.ones((V_ROWS - DV, TK), BF16)
        kf = kc.astype(F32)
        n2 = jnp.max(jnp.dot((kf * kf).astype(BF16), half_ones, preferred_element_type=F32),
                     axis=0, keepdims=True)
        q = q_ref[rows, :].astype(F32) * (DH ** -0.5 * LOG2E)
        own = jnp.min(jnp.dot((q * kf).astype(BF16), half_ones, preferred_element_type=F32),
                      axis=0, keepdims=True)
        skip = tile_lane < c
        for half in range(2):
            in_half = (lane >= DH) if half else (lane < DH)
            in_half_row = (tile_lane >= DH) if half else (tile_lane < DH)
            k_norm = jnp.sqrt(jnp.max(jnp.where(in_half_row, n2, 0.0), axis=1,
                                      keepdims=True)) * KNORM_SLACK
            qb = jnp.where(in_half, q, ctab).astype(BF16).T
            qt_scr[c, half] = qb
            qf = qb[half * DH:(half + 1) * DH, :].astype(F32)
            q_norm = jnp.sqrt(jnp.max(jnp.sum(qf * qf, axis=0, keepdims=True),
                                      axis=1, keepdims=True)) * NORM_SLACK
            own_min = jnp.min(jnp.where(in_half_row, own, jnp.inf), axis=1, keepdims=True)
            upper = q_norm * prefix_max[half] + slope_end
            lower = slope * float(c * TQ) + own_min - OWN_SLACK * (q_norm * k_norm)
            skip = jnp.logical_and(skip, upper + BOUND_MARGIN < lower - EXP2_UNDERFLOW)
            running[half] = jnp.maximum(running[half], k_norm)
            prefix_max[half] = jnp.where(tile_lane == c, running[half], prefix_max[half])
        first_c = jnp.sum(skip.astype(jnp.int32), axis=1, keepdims=True)
        firsts = jnp.where(tile_lane == c, first_c, firsts)
    first_scr[...] = firsts
    acc_scr[...] = jnp.ones(acc_scr.shape, F32)

    lam = (jnp.exp(jnp.sum(lq1_ref[...] * lk1_ref[...], axis=1, keepdims=True))
           - jnp.exp(jnp.sum(lq2_ref[...] * lk2_ref[...], axis=1, keepdims=True))
           + LAM_INIT)

    def scores(i, j, slot, halves=(0, 1)):
        start = pl.multiple_of(j * TK, TK)
        for half in halves:
            s_scr[slot, half] = jnp.dot(kaug_scr[half, pl.ds(start, TK), :], qt_scr[i, half],
                                        preferred_element_type=F32)

    def consume(j, slot, corr, halves=(0, 1)):
        vt = vt_scr[j]
        for half in halves:
            s = s_scr[slot, half]
            if corr is not None:
                s = s + corr
            m_prev = m_scr[half]
            m_new = jnp.maximum(m_prev, jnp.max(s, axis=0, keepdims=True))
            alpha = jnp.exp2(m_prev - m_new)
            p = jnp.exp2(s - m_new).astype(BF16)
            acc_scr[half] = alpha * acc_scr[half] + jnp.dot(
                vt, p, preferred_element_type=F32)
            m_scr[half] = m_new

    def finish(i):
        inv_l0 = 1.0 / acc_scr[0, DV:DV + 1, :]
        inv_l1 = 1.0 / acc_scr[1, DV:DV + 1, :]
        o_t = acc_scr[0, 0:DV, :] * inv_l0 - lam * (acc_scr[1, 0:DV, :] * inv_l1)
        ms = jnp.mean(o_t * o_t, axis=0, keepdims=True)
        y = (o_t * lax.rsqrt(ms + EPS)).T
        row0 = pl.multiple_of(i * TQ, TQ)
        o_ref[pl.ds(row0, TQ), :] = (y * g_ref[...] * (1.0 - LAM_INIT)).astype(o_ref.dtype)

    n_q = seq // TQ

    def q_tile(i, first):
        nxt = jnp.minimum(i + 1, n_q - 1)
        n_past = i - first
        n_rem = n_past - 1

        def start():
            nxt_scr[0] = jnp.sum(jnp.where(tile_lane == nxt, first_scr[...], 0))
            finish(jnp.maximum(i - 1, 0))
            m_scr[...] = jnp.full(m_scr.shape, MASKED, F32)
            acc_scr[...] = jnp.zeros(acc_scr.shape, F32)

        def short_row(n):
            for u in range(1, min(n, 2) + 1):
                scores(i, first + u, u - 1)
            start()
            for u in range(n + 1):
                slot = 2 if u == 0 else (u - 1) % 2
                corr = corr_ref[0] if u == n else None
                for half in range(2):
                    consume(first + u, slot, corr, (half,))
                    if u == 0:
                        scores(nxt, nxt_scr[0], 2, (half,))
                    elif u + 2 <= n:
                        scores(i, first + u + 2, slot, (half,))

        for n in range(SHORT_ROW_TILES + 1):
            pl.when(n_past == n)(functools.partial(short_row, n))

        @pl.when(n_past > SHORT_ROW_TILES)
        def _():
            scores(i, first + 1, 0)
            start()
            consume(first, 2, None)

        def tile_group(j, n):
            scores(i, j + 1, 1)
            for u in range(n - 1):
                for half in range(2):
                    consume(j + u, u % 2, None, (half,))
                    scores(i, j + u + 2, u % 2, (half,))
            consume(j + n - 1, (n - 1) % 2, None)

        def group_loop(base, trips, n):
            def body(t, c):
                tile_group(base + n * t, n)
                return c
            lax.fori_loop(0, trips, body, 0)

        first_nxt = nxt_scr[0]
        n_loop = jnp.where(n_past > SHORT_ROW_TILES, n_rem, 0)
        group_loop(first + 1, n_loop // 4, 4)
        group_loop(first + 1 + 4 * (n_loop // 4), (n_loop % 4) // 2, 2)

        @pl.when(jnp.logical_and(n_past > SHORT_ROW_TILES, n_rem % 2 == 0))
        def _():
            scores(nxt, first_nxt, 2)
            consume(i, 0, corr_ref[0])

        @pl.when(jnp.logical_and(n_past > SHORT_ROW_TILES, n_rem % 2 == 1))
        def _():
            scores(i, i, 1)
            consume(i - 1, 0, None)
            scores(nxt, first_nxt, 2)
            consume(i, 1, corr_ref[0])

        return first_nxt

    scores(0, 0, 2)
    lax.fori_loop(0, n_q, q_tile, jnp.int32(0))
    finish(n_q - 1)


def _bf16_pieces(x, n):
    x = np.asarray(x, np.float32)
    pieces = []
    for _ in range(n):
        p = x.astype(BF16).astype(np.float32)
        pieces.append(p)
        x = x - p
    return pieces


def _attn_tables(seq):
    TQ, TK = TQ_ATT, TK_ATT
    slopes = (np.exp2(-8.0 * np.arange(1, DIFF_HEADS + 1, dtype=np.float64) / DIFF_HEADS)
              * LOG2E).astype(np.float32)
    c1, c2, c3 = _bf16_pieces(slopes, 3)
    ctab = np.zeros((DIFF_HEADS, 1, 128), np.float32)
    pos = np.arange(seq)
    p_hi, p_lo = (pos // 64) * 64, pos % 64
    postab = np.zeros((seq, 128), np.float32)
    for base in (0, DIFF_HEAD_DIM):
        for n, (cp, pp) in enumerate(((c1, p_hi), (c2, p_hi), (c3, p_hi),
                                      (c1, p_lo), (c2, p_lo), (c3, p_lo))):
            ctab[:, 0, base + n] = cp
            postab[:, base + n] = pp
    c = np.arange(TK)[:, None]
    r = np.arange(TQ)[None, :]
    allowed = (c // CHUNK) <= (r // CHUNK)
    ahead = np.where(c > r, 2.0 * (r - c), 0.0)
    corr = np.where(allowed[None], slopes.astype(np.float64)[:, None, None] * ahead[None], MASKED)
    cpos = np.zeros((DIFF_HEADS, 2, 128), np.float64)
    cpos[:, 0, :] = slopes.astype(np.float64)[:, None] * ((np.arange(128) + 1) * TK - 1)[None, :]
    cpos[:, 1, :] = slopes.astype(np.float64)[:, None]
    return (jnp.asarray(postab, BF16), jnp.asarray(ctab, F32), jnp.asarray(cpos, F32),
            jnp.asarray(corr, F32))


def _diff_attention(proj, batch, seq, lq1, lk1, lq2, lk2, subln_g):
    TQ, TK = TQ_ATT, TK_ATT
    assert TQ == TK and seq % TK == 0
    assert seq // TK <= 128
    postab, ctab, cpos, corr = _attn_tables(seq)
    small = lambda shape: pl.BlockSpec(shape, lambda b, h: (0,) * len(shape))
    return pl.pallas_call(
        _attn_kernel,
        grid=(batch, DIFF_HEADS),
        in_specs=[
            pl.BlockSpec((seq, 128), lambda b, h: (b, COL_DQ + h)),
            pl.BlockSpec((seq, 128), lambda b, h: (b, COL_DK + h)),
            pl.BlockSpec((seq, 128), lambda b, h: (b, COL_DV + h)),
            pl.BlockSpec((seq, 128), lambda b, h: (0, 0), pipeline_mode=pl.Buffered(1)),
            pl.BlockSpec((1, 1, 128), lambda b, h: (h, 0, 0)),
            pl.BlockSpec((1, 2, 128), lambda b, h: (h, 0, 0)),
            pl.BlockSpec((1, TK, TQ), lambda b, h: (h, 0, 0)),
            small((1, DIFF_HEAD_DIM)), small((1, DIFF_HEAD_DIM)),
            small((1, DIFF_HEAD_DIM)), small((1, DIFF_HEAD_DIM)),
            small((1, DIFF_V_DIM)),
        ],
        out_specs=pl.BlockSpec((seq, DIFF_V_DIM), lambda b, h: (b, h)),
        out_shape=jax.ShapeDtypeStruct((batch * seq, DIFF_WIDTH), BF16),
        scratch_shapes=[
            pltpu.VMEM((2, seq, 128), BF16),
            pltpu.VMEM((seq // TK, V_ROWS, TK), BF16),
            pltpu.VMEM((seq // TQ, 2, 128, TQ), BF16),
            pltpu.VMEM((1, 128), jnp.int32),
            pltpu.SMEM((1,), jnp.int32),
            pltpu.VMEM((3, 2, TK, TQ), F32),
            pltpu.VMEM((2, 1, TQ), F32),
            pltpu.VMEM((2, V_ROWS, TQ), F32),
        ],
        compiler_params=pltpu.CompilerParams(
            dimension_semantics=("arbitrary", "arbitrary"), vmem_limit_bytes=VMEM_LIMIT),
        name="diff_attn",
    )(proj, proj, proj, postab, ctab, cpos, corr, lq1, lk1, lq2, lk2, subln_g)


def _ret_kernel(q_ref, k_ref, v_ref, gate_ref, dmat_ref, qdec_ref, kdec_ref, cdec_ref,
                gn_ref, o_ref, state_scr):
    T = T_RET

    @pl.when(pl.program_id(1) == 0)
    def _():
        state_scr[...] = jnp.zeros(state_scr.shape, F32)

    lane = lax.broadcasted_iota(jnp.int32, (T, 128), 1)
    heads = range(RET_HEADS)
    for r in range(RET_BLOCKS):
        rows = slice(r * T, (r + 1) * T)
        qp, km, v, s, outs = {}, {}, {}, {}, {}
        for h in heads:
            pair, half = divmod(h, 2)
            qp[h] = q_ref[rows, 128 * pair:128 * (pair + 1)]
            kp = k_ref[rows, 128 * pair:128 * (pair + 1)]
            in_head = (lane >= RET_KEY_DIM) if half else (lane < RET_KEY_DIM)
            km[h] = jnp.where(in_head, kp, jnp.zeros_like(kp))
            v[h] = v_ref[rows, 128 * h:128 * (h + 1)]
            s[h] = lax.dot_general(qp[h], km[h], (((1,), (1,)), ((), ())),
                                   preferred_element_type=F32)
        for h in heads:
            state = state_scr[h]
            cross = jnp.dot(qp[h], state.astype(BF16), preferred_element_type=F32)
            intra = jnp.dot((s[h] * dmat_ref[h]).astype(BF16), v[h],
                            preferred_element_type=F32)
            outs[h] = intra + qdec_ref[h] * cross
            vd = (v[h].astype(F32) * kdec_ref[h]).astype(BF16)
            state_scr[h] = cdec_ref[h] * state + lax.dot_general(
                km[h], vd, (((0,), (0,)), ((), ())), preferred_element_type=F32)
        for h in heads:
            cols = slice(128 * h, 128 * (h + 1))
            o = outs[h]
            mu = jnp.mean(o, axis=-1, keepdims=True)
            d = o - mu
            var = jnp.mean(d * d, axis=-1, keepdims=True)
            y = d * lax.rsqrt(var + EPS) * gn_ref[:, cols]
            gate = gate_ref[rows, cols].astype(F32)
            y = y * (gate * (1.0 / (1.0 + jnp.exp(-gate))))
            o_ref[rows, cols] = y.astype(o_ref.dtype)


def _ret_tables():
    T = T_RET
    log_g = np.log1p(-np.exp2(-5.0 - np.arange(RET_HEADS, dtype=np.float64)))
    n = np.arange(T, dtype=np.float64)
    kscale = RET_KEY_DIM ** -0.5
    allowed = (np.arange(T)[None, :] // CHUNK) <= (np.arange(T)[:, None] // CHUNK)
    dmat = np.where(allowed[None],
                    np.exp(log_g[:, None, None] * np.abs(n[:, None] - n[None, :])[None]), 0.0) * kscale
    qdec = np.exp(log_g[:, None] * (n + 1.0)[None]) * kscale
    kdec = np.exp(log_g[:, None] * (T - 1.0 - n)[None])
    cdec = np.exp(log_g * T)
    qdec = np.broadcast_to(qdec[:, :, None], (RET_HEADS, T, 128))
    kdec = np.broadcast_to(kdec[:, :, None], (RET_HEADS, T, 128))
    cdec = np.broadcast_to(cdec[:, None, None], (RET_HEADS, 1, 128))
    return tuple(jnp.asarray(a, F32) for a in (dmat, qdec, kdec, cdec))


def _retention(proj, batch, seq, gn_g):
    T = T_RET
    TS = T * RET_BLOCKS
    nt = seq // TS
    dmat, qdec, kdec, cdec = _ret_tables()
    const = lambda shape: pl.BlockSpec(shape, lambda b, t: (0,) * len(shape))
    return pl.pallas_call(
        _ret_kernel,
        grid=(batch, nt),
        in_specs=[
            pl.BlockSpec((TS, 256), lambda b, t: (b * nt + t, COLB_RQ)),
            pl.BlockSpec((TS, 256), lambda b, t: (b * nt + t, COLB_RK)),
            pl.BlockSpec((TS, 512), lambda b, t: (b * nt + t, COLB_RV)),
            pl.BlockSpec((TS, 512), lambda b, t: (b * nt + t, COLB_RG)),
            const((RET_HEADS, T, T)), const((RET_HEADS, T, 128)),
            const((RET_HEADS, T, 128)), const((RET_HEADS, 1, 128)),
            const((1, RET_WIDTH)),
        ],
        out_specs=pl.BlockSpec((TS, RET_WIDTH), lambda b, t: (b * nt + t, 0)),
        out_shape=jax.ShapeDtypeStruct((batch * seq, RET_WIDTH), BF16),
        scratch_shapes=[pltpu.VMEM((RET_HEADS, 128, 128), F32)],
        compiler_params=pltpu.CompilerParams(
            dimension_semantics=("arbitrary", "arbitrary"), vmem_limit_bytes=VMEM_LIMIT),
        name="retention",
    )(proj, proj, proj, proj, dmat, qdec, kdec, cdec, gn_g)


def _ffn_kernel(tiles_per_seq, x_ref, od_ref, or_ref, wo_ref, g2_ref, wup_ref,
                cw_ref, cb_ref, wd_ref, gf_ref, o_ref, xn_scr, h_scr, halo_scr):
    TM = TM_FFN

    @pl.when(pl.program_id(0) % tiles_per_seq == 0)
    def _():
        halo_scr[...] = jnp.zeros(halo_scr.shape, F32)

    SUB = TM // 2
    row = lax.broadcasted_iota(jnp.int32, (SUB, TF), 0)
    halos = [halo_scr[c] for c in range(NF)]
    groups = (slice(0, SUB), slice(SUB, TM))

    def out_proj(rows):
        x1 = (x_ref[rows, :]
              + jnp.dot(od_ref[rows, :], wo_ref[0:DIFF_WIDTH, :], preferred_element_type=F32)
              + jnp.dot(or_ref[rows, :], wo_ref[DIFF_WIDTH:, :], preferred_element_type=F32))
        o_ref[rows, :] = x1

    def pre_norm(rows):
        xn_scr[rows, :] = _rms(o_ref[rows, :], g2_ref[...]).astype(BF16)

    def chunk(rows, c):
        xn = xn_scr[rows, :]
        cols = slice(c * TF, (c + 1) * TF)
        gate_cols = slice(D_FF + c * TF, D_FF + (c + 1) * TF)
        a = jnp.dot(xn, wup_ref[:, cols], preferred_element_type=F32)
        b = jnp.dot(xn, wup_ref[:, gate_cols], preferred_element_type=F32)
        halo = halos[c]
        halos[c] = a[SUB - 8:SUB, :]
        a1 = jnp.where(row == 0, halo[7:8, :], pltpu.roll(a, 1, 0))
        a2 = pltpu.roll(a, 2, 0)
        a2 = jnp.where(row == 0, halo[6:7, :], jnp.where(row == 1, halo[7:8, :], a2))
        u = (cw_ref[0:1, cols] * a2 + cw_ref[1:2, cols] * a1 + cw_ref[2:3, cols] * a
             + cb_ref[:, cols])
        gelu = 0.5 * u * (1.0 + jnp.tanh(0.7978845608028654 * (u + 0.044715 * (u * u * u))))
        h_scr[rows, cols] = (gelu * b).astype(BF16)

    def down_proj(rows):
        o_ref[rows, :] = o_ref[rows, :] + jnp.dot(h_scr[rows, :], wd_ref[...],
                                                  preferred_element_type=F32)

    def post_norm(rows):
        o_ref[rows, :] = _rms(o_ref[rows, :], gf_ref[...])

    g0, g1 = groups
    out_proj(g0)
    out_proj(g1)
    pre_norm(g0)
    chunk(g0, 0)
    pre_norm(g1)
    for c in range(1, NF):
        chunk(g0, c)
    down_proj(g0)
    chunk(g1, 0)
    post_norm(g0)
    for c in range(1, NF):
        chunk(g1, c)
    down_proj(g1)
    post_norm(g1)
    for c in range(NF):
        halo_scr[c] = halos[c]


def _ffn(x2, o_diff, o_ret, wo, g2, wup, cw, cb, wd, gf, seq):
    n = x2.shape[0]
    TM = TM_FFN
    const = lambda shape: pl.BlockSpec(shape, lambda i: (0,) * len(shape),
                                       pipeline_mode=pl.Buffered(1))
    return pl.pallas_call(
        functools.partial(_ffn_kernel, seq // TM),
        grid=(n // TM,),
        in_specs=[
            pl.BlockSpec((TM, D_MODEL), lambda i: (i, 0)),
            pl.BlockSpec((TM, DIFF_WIDTH), lambda i: (i, 0)),
            pl.BlockSpec((TM, RET_WIDTH), lambda i: (i, 0)),
            const((D_MODEL, D_MODEL)), const((1, D_MODEL)),
            const((D_MODEL, 2 * D_FF)),
            const((3, D_FF)), const((1, D_FF)),
            const((D_FF, D_MODEL)), const((1, D_MODEL)),
        ],
        out_specs=pl.BlockSpec((TM, D_MODEL), lambda i: (i, 0)),
        out_shape=jax.ShapeDtypeStruct((n, D_MODEL), F32),
        scratch_shapes=[
            pltpu.VMEM((TM, D_MODEL), BF16),
            pltpu.VMEM((TM, D_FF), BF16),
            pltpu.VMEM((NF, 8, TF), F32),
        ],
        compiler_params=pltpu.CompilerParams(
            dimension_semantics=("arbitrary",), vmem_limit_bytes=VMEM_LIMIT),
        name="outproj_ffn",
    )(x2, o_diff, o_ret, wo, g2, wup, cw, cb, wd, gf)


def kernel(x, norm_mix_g, w_in, lambda_q1, lambda_k1, lambda_q2, lambda_k2, diff_subln_g,
           ret_gn_g, w_out, norm_ffn_g, w_up, conv_w, conv_b, w_down, final_norm_g):
    batch, seq, _ = x.shape
    x2 = x.reshape(batch * seq, D_MODEL)

    proj = _inproj(x2, norm_mix_g, w_in[0].astype(BF16))
    o_diff = _diff_attention(proj, batch, seq, lambda_q1, lambda_k1, lambda_q2, lambda_k2,
                             diff_subln_g)
    o_ret = _retention(proj, batch, seq, ret_gn_g)

    y = _ffn(x2, o_diff, o_ret, w_out[0].astype(BF16), norm_ffn_g, w_up[0].astype(BF16),
             conv_w[0], conv_b[0].reshape(1, D_FF), w_down[0].astype(BF16),
             final_norm_g.reshape(1, D_MODEL), seq)
    return y.reshape(batch, seq, D_MODEL)
```

```python
import functools
import math

import jax
import jax.numpy as jnp
import numpy as np
from jax import lax
from jax.experimental import pallas as pl
from jax.experimental.pallas import tpu as pltpu

D_MODEL = 1024
CHUNK = 64
DIFF_HEADS = 4
DIFF_HEAD_DIM = 64
DIFF_V_DIM = 128
DIFF_WIDTH = 512
RET_HEADS = 4
RET_KEY_DIM = 64
RET_V_DIM = 128
RET_WIDTH = 512
D_FF = 2816
IN_WIDTH = 3072
EPS = 1e-6
LAM_INIT = 0.8 - 0.6 * math.exp(-0.3 * 0)
LOG2E = math.log2(math.e)
MASKED = -1e30
LANES = 128
GELU_C = math.sqrt(2.0 / math.pi)
GELU_A = 0.044715

COL_DQ, COL_DK, COL_DV = 0, 4, 8
COLB_RQ, COLB_RK = 6, 7
COLB_RV, COLB_RG = 4, 5

TM_IN = 1024
TQ_ATT = 512
TK_ATT = 512
V_ROWS = DIFF_V_DIM + 16
EXP2_UNDERFLOW = 150.0
BOUND_MARGIN = 2.0
NORM_SLACK = 1.001
KNORM_SLACK = 1.002
OWN_SLACK = 2.0 ** -7
T_RET = 256
RET_BLOCKS = 8
TM_FFN = 1024
TF = 256
NF = D_FF // TF
VMEM_LIMIT = 56 * 1024 * 1024

BF16 = jnp.bfloat16
F32 = jnp.float32


def _rms(x, g):
    ms = jnp.mean(x * x, axis=-1, keepdims=True)
    return x * lax.rsqrt(ms + EPS) * g


def _inproj_kernel(x_ref, g_ref, w_ref, o_ref):
    h = _rms(x_ref[...], g_ref[...]).astype(BF16)
    o_ref[...] = jnp.dot(h, w_ref[...], preferred_element_type=F32).astype(o_ref.dtype)


def _inproj(x2, g, w_bf16):
    n = x2.shape[0]
    return pl.pallas_call(
        _inproj_kernel,
        grid=(n // TM_IN,),
        in_specs=[
            pl.BlockSpec((TM_IN, D_MODEL), lambda i: (i, 0)),
            pl.BlockSpec((1, D_MODEL), lambda i: (0, 0)),
            pl.BlockSpec((D_MODEL, IN_WIDTH), lambda i: (0, 0), pipeline_mode=pl.Buffered(1)),
        ],
        out_specs=pl.BlockSpec((TM_IN, IN_WIDTH), lambda i: (i, 0)),
        out_shape=jax.ShapeDtypeStruct((n, IN_WIDTH), BF16),
        compiler_params=pltpu.CompilerParams(
            dimension_semantics=("arbitrary",), vmem_limit_bytes=VMEM_LIMIT),
        name="inproj",
    )(x2, g, w_bf16)


def _attn_kernel(q_ref, k_ref, v_ref, postab_ref, ctab_ref, cpos_ref, corr_ref,
                 lq1_ref, lk1_ref, lq2_ref, lk2_ref, g_ref, o_ref,
                 kaug_scr, vt_scr, qt_scr, first_scr, nxt_scr, s_scr, m_scr, acc_scr):
    TQ, TK = TQ_ATT, TK_ATT
    DV, DH = DIFF_V_DIM, DIFF_HEAD_DIM
    seq = k_ref.shape[0]
    n_tiles = seq // TK
    tile_lane = lax.broadcasted_iota(jnp.int32, (1, LANES), 1)
    lane = lax.broadcasted_iota(jnp.int32, (TK, LANES), 1)
    ctab = ctab_ref[0]
    slope_end = cpos_ref[0, 0:1, :]
    slope = cpos_ref[0, 1:2, :]

    ones_row = lax.broadcasted_iota(jnp.int32, (LANES, LANES), 0)
    ones_col = lax.broadcasted_iota(jnp.int32, (LANES, LANES), 1)
    half_ones = jnp.where((ones_row < DH) == (ones_col < DH), 1.0, 0.0).astype(BF16)
    prefix_max = [jnp.zeros((1, LANES), F32)] * 2
    running = [jnp.zeros((1, 1), F32)] * 2
    firsts = jnp.zeros((1, LANES), jnp.int32)
    for c in range(n_tiles):
        rows = slice(c * TK, (c + 1) * TK)
        kc = k_ref[rows, :]
        pt = postab_ref[rows, :]
        kaug_scr[0, rows, :] = jnp.where(lane < DH, kc, pt)
        kaug_scr[1, rows, :] = jnp.where(lane >= DH, kc, pt)
        vt_scr[c, 0:DV, :] = v_ref[rows, :].T
        vt_scr[c, DV:, :] = jnp.ones((V_ROWS - DV, TK), BF16)
        kf = kc.astype(F32)
        n2 = jnp.max(jnp.dot((kf * kf).astype(BF16), half_ones, preferred_element_type=F32),
                     axis=0, keepdims=True)
        q = q_ref[rows, :].astype(F32) * (DH ** -0.5 * LOG2E)
        own = jnp.min(jnp.dot((q * kf).astype(BF16), half_ones, preferred_element_type=F32),
                      axis=0, keepdims=True)
        skip = tile_lane < c
        for half in range(2):
            in_half = (lane >= DH) if half else (lane < DH)
            in_half_row = (tile_lane >= DH) if half else (tile_lane < DH)
            k_norm = jnp.sqrt(jnp.max(jnp.where(in_half_row, n2, 0.0), axis=1,
                                      keepdims=True)) * KNORM_SLACK
            qb = jnp.where(in_half, q, ctab).astype(BF16).T
            qt_scr[c, half] = qb
            qf = qb[half * DH:(half + 1) * DH, :].astype(F32)
            q_norm = jnp.sqrt(jnp.max(jnp.sum(qf * qf, axis=0, keepdims=True),
                                      axis=1, keepdims=True)) * NORM_SLACK
            own_min = jnp.min(jnp.where(in_half_row, own, jnp.inf), axis=1, keepdims=True)
            upper = q_norm * prefix_max[half] + slope_end
            lower = slope * float(c * TQ) + own_min - OWN_SLACK * (q_norm * k_norm)
            skip = jnp.logical_and(skip, upper + BOUND_MARGIN < lower - EXP2_UNDERFLOW)
            running[half] = jnp.maximum(running[half], k_norm)
            prefix_max[half] = jnp.where(tile_lane == c, running[half], prefix_max[half])
        first_c = jnp.sum(skip.astype(jnp.int32), axis=1, keepdims=True)
        firsts = jnp.where(tile_lane == c, first_c, firsts)
    first_scr[...] = firsts
    acc_scr[...] = jnp.ones(acc_scr.shape, F32)

    lam = (jnp.exp(jnp.sum(lq1_ref[...] * lk1_ref[...], axis=1, keepdims=True))
           - jnp.exp(jnp.sum(lq2_ref[...] * lk2_ref[...], axis=1, keepdims=True))
           + LAM_INIT)

    def scores(i, j, slot, halves=(0, 1)):
        start = pl.multiple_of(j * TK, TK)
        for half in halves:
            s_scr[slot, half] = jnp.dot(kaug_scr[half, pl.ds(start, TK), :], qt_scr[i, half],
                                        preferred_element_type=F32)

    def consume(j, slot, corr, halves=(0, 1)):
        vt = vt_scr[j]
        for half in halves:
            s = s_scr[slot, half]
            if corr is not None:
                s = s + corr
            m_prev = m_scr[half]
            m_new = jnp.maximum(m_prev, jnp.max(s, axis=0, keepdims=True))
            alpha = jnp.exp2(m_prev - m_new)
            p = jnp.exp2(s - m_new).astype(BF16)
            acc_scr[half] = alpha * acc_scr[half] + jnp.dot(
                vt, p, preferred_element_type=F32)
            m_scr[half] = m_new

    def finish(i):
        inv_l0 = 1.0 / acc_scr[0, DV:DV + 1, :]
        inv_l1 = 1.0 / acc_scr[1, DV:DV + 1, :]
        o_t = acc_scr[0, 0:DV, :] * inv_l0 - lam * (acc_scr[1, 0:DV, :] * inv_l1)
        ms = jnp.mean(o_t * o_t, axis=0, keepdims=True)
        y = (o_t * lax.rsqrt(ms + EPS)).T
        row0 = pl.multiple_of(i * TQ, TQ)
        o_ref[pl.ds(row0, TQ), :] = (y * g_ref[...] * (1.0 - LAM_INIT)).astype(o_ref.dtype)

    n_q = seq // TQ

    def q_tile(i, first):
        nxt = jnp.minimum(i + 1, n_q - 1)
        n_past = i - first
        n_rem = n_past - 1

        def start():
            nxt_scr[0] = jnp.sum(jnp.where(tile_lane == nxt, first_scr[...], 0))
            finish(jnp.maximum(i - 1, 0))
            m_scr[...] = jnp.full(m_scr.shape, MASKED, F32)
            acc_scr[...] = jnp.zeros(acc_scr.shape, F32)

        @pl.when(n_past == 0)
        def _():
            start()
            for half in range(2):
                consume(i, 2, corr_ref[0], (half,))
                scores(nxt, nxt_scr[0], 2, (half,))

        @pl.when(n_past == 1)
        def _():
            scores(i, i, 0)
            start()
            for half in range(2):
                consume(first, 2, None, (half,))
                scores(nxt, nxt_scr[0], 2, (half,))
            consume(i, 0, corr_ref[0])

        @pl.when(n_past > 1)
        def _():
            scores(i, first + 1, 0)
            start()
            consume(first, 2, None)

        def tile_group(j, n):
            scores(i, j + 1, 1)
            for u in range(n - 1):
                for half in range(2):
                    consume(j + u, u % 2, None, (half,))
                    scores(i, j + u + 2, u % 2, (half,))
            consume(j + n - 1, (n - 1) % 2, None)

        def group_loop(base, trips, n):
            def body(t, c):
                tile_group(base + n * t, n)
                return c
            lax.fori_loop(0, trips, body, 0)

        first_nxt = nxt_scr[0]
        n_loop = jnp.maximum(n_rem, 0)
        group_loop(first + 1, n_loop // 4, 4)
        group_loop(first + 1 + 4 * (n_loop // 4), (n_loop % 4) // 2, 2)

        @pl.when(jnp.logical_and(n_past > 1, n_rem % 2 == 0))
        def _():
            scores(nxt, first_nxt, 2)
            consume(i, 0, corr_ref[0])

        @pl.when(jnp.logical_and(n_past > 1, n_rem % 2 == 1))
        def _():
            scores(i, i, 1)
            consume(i - 1, 0, None)
            scores(nxt, first_nxt, 2)
            consume(i, 1, corr_ref[0])

        return first_nxt

    scores(0, 0, 2)
    lax.fori_loop(0, n_q, q_tile, jnp.int32(0))
    finish(n_q - 1)


def _bf16_pieces(x, n):
    x = np.asarray(x, np.float32)
    pieces = []
    for _ in range(n):
        p = x.astype(BF16).astype(np.float32)
        pieces.append(p)
        x = x - p
    return pieces


def _attn_tables(seq):
    TQ, TK = TQ_ATT, TK_ATT
    slopes = (np.exp2(-8.0 * np.arange(1, DIFF_HEADS + 1, dtype=np.float64) / DIFF_HEADS)
              * LOG2E).astype(np.float32)
    c1, c2, c3 = _bf16_pieces(slopes, 3)
    ctab = np.zeros((DIFF_HEADS, 1, LANES), np.float32)
    pos = np.arange(seq)
    p_hi, p_lo = (pos // 64) * 64, pos % 64
    postab = np.zeros((seq, LANES), np.float32)
    for base in (0, DIFF_HEAD_DIM):
        for n, (cp, pp) in enumerate(((c1, p_hi), (c2, p_hi), (c3, p_hi),
                                      (c1, p_lo), (c2, p_lo), (c3, p_lo))):
            ctab[:, 0, base + n] = cp
            postab[:, base + n] = pp
    c = np.arange(TK)[:, None]
    r = np.arange(TQ)[None, :]
    allowed = (c // CHUNK) <= (r // CHUNK)
    ahead = np.where(c > r, 2.0 * (r - c), 0.0)
    corr = np.where(allowed[None], slopes.astype(np.float64)[:, None, None] * ahead[None], MASKED)
    cpos = np.zeros((DIFF_HEADS, 2, LANES), np.float64)
    cpos[:, 0, :] = slopes.astype(np.float64)[:, None] * ((np.arange(LANES) + 1) * TK - 1)[None, :]
    cpos[:, 1, :] = slopes.astype(np.float64)[:, None]
    return (jnp.asarray(postab, BF16), jnp.asarray(ctab, F32), jnp.asarray(cpos, F32),
            jnp.asarray(corr, F32))


def _diff_attention(proj, batch, seq, lq1, lk1, lq2, lk2, subln_g):
    TQ, TK = TQ_ATT, TK_ATT
    assert TQ == TK and seq % TK == 0
    assert seq // TK <= LANES
    postab, ctab, cpos, corr = _attn_tables(seq)
    small = lambda shape: pl.BlockSpec(shape, lambda b, h: (0,) * len(shape))
    return pl.pallas_call(
        _attn_kernel,
        grid=(batch, DIFF_HEADS),
        in_specs=[
            pl.BlockSpec((seq, LANES), lambda b, h: (b, COL_DQ + h)),
            pl.BlockSpec((seq, LANES), lambda b, h: (b, COL_DK + h)),
            pl.BlockSpec((seq, LANES), lambda b, h: (b, COL_DV + h)),
            pl.BlockSpec((seq, LANES), lambda b, h: (0, 0), pipeline_mode=pl.Buffered(1)),
            pl.BlockSpec((1, 1, LANES), lambda b, h: (h, 0, 0)),
            pl.BlockSpec((1, 2, LANES), lambda b, h: (h, 0, 0)),
            pl.BlockSpec((1, TK, TQ), lambda b, h: (h, 0, 0)),
            small((1, DIFF_HEAD_DIM)), small((1, DIFF_HEAD_DIM)),
            small((1, DIFF_HEAD_DIM)), small((1, DIFF_HEAD_DIM)),
            small((1, DIFF_V_DIM)),
        ],
        out_specs=pl.BlockSpec((seq, DIFF_V_DIM), lambda b, h: (b, h)),
        out_shape=jax.ShapeDtypeStruct((batch * seq, DIFF_WIDTH), BF16),
        scratch_shapes=[
            pltpu.VMEM((2, seq, LANES), BF16),
            pltpu.VMEM((seq // TK, V_ROWS, TK), BF16),
            pltpu.VMEM((seq // TQ, 2, LANES, TQ), BF16),
            pltpu.VMEM((1, LANES), jnp.int32),
            pltpu.SMEM((1,), jnp.int32),
            pltpu.VMEM((3, 2, TK, TQ), F32),
            pltpu.VMEM((2, 1, TQ), F32),
            pltpu.VMEM((2, V_ROWS, TQ), F32),
        ],
        compiler_params=pltpu.CompilerParams(
            dimension_semantics=("arbitrary", "arbitrary"), vmem_limit_bytes=VMEM_LIMIT),
        name="diff_attn",
    )(proj, proj, proj, postab, ctab, cpos, corr, lq1, lk1, lq2, lk2, subln_g)


def _ret_kernel(q_ref, k_ref, v_ref, gate_ref, dmat_ref, qdec_ref, kdec_ref, cdec_ref,
                gn_ref, o_ref, state_scr):
    T = T_RET

    @pl.when(pl.program_id(1) == 0)
    def _():
        state_scr[...] = jnp.zeros(state_scr.shape, F32)

    lane = lax.broadcasted_iota(jnp.int32, (T, LANES), 1)
    heads = range(RET_HEADS)
    for r in range(RET_BLOCKS):
        rows = slice(r * T, (r + 1) * T)
        qp, km, v, s, outs = {}, {}, {}, {}, {}
        for h in heads:
            pair, half = divmod(h, 2)
            qp[h] = q_ref[rows, LANES * pair:LANES * (pair + 1)]
            kp = k_ref[rows, LANES * pair:LANES * (pair + 1)]
            in_head = (lane >= RET_KEY_DIM) if half else (lane < RET_KEY_DIM)
            km[h] = jnp.where(in_head, kp, jnp.zeros_like(kp))
            v[h] = v_ref[rows, RET_V_DIM * h:RET_V_DIM * (h + 1)]
            s[h] = lax.dot_general(qp[h], km[h], (((1,), (1,)), ((), ())),
                                   preferred_element_type=F32)
        for h in heads:
            state = state_scr[h]
            cross = jnp.dot(qp[h], state.astype(BF16), preferred_element_type=F32)
            intra = jnp.dot((s[h] * dmat_ref[h]).astype(BF16), v[h],
                            preferred_element_type=F32)
            outs[h] = intra + qdec_ref[h] * cross
            vd = (v[h].astype(F32) * kdec_ref[h]).astype(BF16)
            state_scr[h] = cdec_ref[h] * state + lax.dot_general(
                km[h], vd, (((0,), (0,)), ((), ())), preferred_element_type=F32)
        for h in heads:
            cols = slice(RET_V_DIM * h, RET_V_DIM * (h + 1))
            o = outs[h]
            mu = jnp.mean(o, axis=-1, keepdims=True)
            d = o - mu
            var = jnp.mean(d * d, axis=-1, keepdims=True)
            y = d * lax.rsqrt(var + EPS) * gn_ref[:, cols]
            gate = gate_ref[rows, cols].astype(F32)
            y = y * (gate * (1.0 / (1.0 + jnp.exp(-gate))))
            o_ref[rows, cols] = y.astype(o_ref.dtype)


def _ret_tables():
    T = T_RET
    log_g = np.log1p(-np.exp2(-5.0 - np.arange(RET_HEADS, dtype=np.float64)))
    n = np.arange(T, dtype=np.float64)
    kscale = RET_KEY_DIM ** -0.5
    allowed = (np.arange(T)[None, :] // CHUNK) <= (np.arange(T)[:, None] // CHUNK)
    dmat = np.where(allowed[None],
                    np.exp(log_g[:, None, None] * np.abs(n[:, None] - n[None, :])[None]), 0.0) * kscale
    qdec = np.exp(log_g[:, None] * (n + 1.0)[None]) * kscale
    kdec = np.exp(log_g[:, None] * (T - 1.0 - n)[None])
    cdec = np.exp(log_g * T)
    qdec = np.broadcast_to(qdec[:, :, None], (RET_HEADS, T, RET_V_DIM))
    kdec = np.broadcast_to(kdec[:, :, None], (RET_HEADS, T, RET_V_DIM))
    cdec = np.broadcast_to(cdec[:, None, None], (RET_HEADS, 1, RET_V_DIM))
    return tuple(jnp.asarray(a, F32) for a in (dmat, qdec, kdec, cdec))


def _retention(proj, batch, seq, gn_g):
    T = T_RET
    TS = T * RET_BLOCKS
    nt = seq // TS
    dmat, qdec, kdec, cdec = _ret_tables()
    const = lambda shape: pl.BlockSpec(shape, lambda b, t: (0,) * len(shape))
    return pl.pallas_call(
        _ret_kernel,
        grid=(batch, nt),
        in_specs=[
            pl.BlockSpec((TS, 256), lambda b, t: (b * nt + t, COLB_RQ)),
            pl.BlockSpec((TS, 256), lambda b, t: (b * nt + t, COLB_RK)),
            pl.BlockSpec((TS, 512), lambda b, t: (b * nt + t, COLB_RV)),
            pl.BlockSpec((TS, 512), lambda b, t: (b * nt + t, COLB_RG)),
            const((RET_HEADS, T, T)), const((RET_HEADS, T, RET_V_DIM)),
            const((RET_HEADS, T, RET_V_DIM)), const((RET_HEADS, 1, RET_V_DIM)),
            const((1, RET_WIDTH)),
        ],
        out_specs=pl.BlockSpec((TS, RET_WIDTH), lambda b, t: (b * nt + t, 0)),
        out_shape=jax.ShapeDtypeStruct((batch * seq, RET_WIDTH), BF16),
        scratch_shapes=[pltpu.VMEM((RET_HEADS, LANES, RET_V_DIM), F32)],
        compiler_params=pltpu.CompilerParams(
            dimension_semantics=("arbitrary", "arbitrary"), vmem_limit_bytes=VMEM_LIMIT),
        name="retention",
    )(proj, proj, proj, proj, dmat, qdec, kdec, cdec, gn_g)


def _ffn_kernel(tiles_per_seq, x_ref, od_ref, or_ref, wo_ref, g2_ref, wup_ref,
                cw_ref, cb_ref, wd_ref, gf_ref, o_ref, xn_scr, h_scr, halo_scr):
    TM = TM_FFN

    @pl.when(pl.program_id(0) % tiles_per_seq == 0)
    def _():
        halo_scr[...] = jnp.zeros(halo_scr.shape, F32)

    SUB = TM // 2
    row = lax.broadcasted_iota(jnp.int32, (SUB, TF), 0)
    halos = [halo_scr[c] for c in range(NF)]
    groups = (slice(0, SUB), slice(SUB, TM))

    def out_proj(rows):
        x1 = (x_ref[rows, :]
              + jnp.dot(od_ref[rows, :], wo_ref[0:DIFF_WIDTH, :], preferred_element_type=F32)
              + jnp.dot(or_ref[rows, :], wo_ref[DIFF_WIDTH:, :], preferred_element_type=F32))
        o_ref[rows, :] = x1

    def pre_norm(rows):
        xn_scr[rows, :] = _rms(o_ref[rows, :], g2_ref[...]).astype(BF16)

    def chunk(rows, c):
        xn = xn_scr[rows, :]
        cols = slice(c * TF, (c + 1) * TF)
        gate_cols = slice(D_FF + c * TF, D_FF + (c + 1) * TF)
        a = jnp.dot(xn, wup_ref[:, cols], preferred_element_type=F32)
        b = jnp.dot(xn, wup_ref[:, gate_cols], preferred_element_type=F32)
        halo = halos[c]
        halos[c] = a[SUB - 8:SUB, :]
        a1 = jnp.where(row == 0, halo[7:8, :], pltpu.roll(a, 1, 0))
        a2 = pltpu.roll(a, 2, 0)
        a2 = jnp.where(row == 0, halo[6:7, :], jnp.where(row == 1, halo[7:8, :], a2))
        u = (cw_ref[0:1, cols] * a2 + cw_ref[1:2, cols] * a1 + cw_ref[2:3, cols] * a
             + cb_ref[:, cols])
        gelu = 0.5 * u * (1.0 + jnp.tanh(u * (GELU_C + (GELU_C * GELU_A) * (u * u))))
        h_scr[rows, cols] = (gelu * b).astype(BF16)

    def down_proj(rows):
        o_ref[rows, :] = o_ref[rows, :] + jnp.dot(h_scr[rows, :], wd_ref[...],
                                                  preferred_element_type=F32)

    def post_norm(rows):
        o_ref[rows, :] = _rms(o_ref[rows, :], gf_ref[...])

    g0, g1 = groups
    out_proj(g0)
    out_proj(g1)
    pre_norm(g0)
    chunk(g0, 0)
    pre_norm(g1)
    for c in range(1, NF):
        chunk(g0, c)
    down_proj(g0)
    chunk(g1, 0)
    post_norm(g0)
    for c in range(1, NF):
        chunk(g1, c)
    down_proj(g1)
    post_norm(g1)
    for c in range(NF):
        halo_scr[c] = halos[c]


def _ffn(x2, o_diff, o_ret, wo, g2, wup, cw, cb, wd, gf, seq):
    n = x2.shape[0]
    TM = TM_FFN
    const = lambda shape: pl.BlockSpec(shape, lambda i: (0,) * len(shape),
                                       pipeline_mode=pl.Buffered(1))
    return pl.pallas_call(
        functools.partial(_ffn_kernel, seq // TM),
        grid=(n // TM,),
        in_specs=[
            pl.BlockSpec((TM, D_MODEL), lambda i: (i, 0)),
            pl.BlockSpec((TM, DIFF_WIDTH), lambda i: (i, 0)),
            pl.BlockSpec((TM, RET_WIDTH), lambda i: (i, 0)),
            const((D_MODEL, D_MODEL)), const((1, D_MODEL)),
            const((D_MODEL, 2 * D_FF)),
            const((3, D_FF)), const((1, D_FF)),
            const((D_FF, D_MODEL)), const((1, D_MODEL)),
        ],
        out_specs=pl.BlockSpec((TM, D_MODEL), lambda i: (i, 0)),
        out_shape=jax.ShapeDtypeStruct((n, D_MODEL), F32),
        scratch_shapes=[
            pltpu.VMEM((TM, D_MODEL), BF16),
            pltpu.VMEM((TM, D_FF), BF16),
            pltpu.VMEM((NF, 8, TF), F32),
        ],
        compiler_params=pltpu.CompilerParams(
            dimension_semantics=("arbitrary",), vmem_limit_bytes=VMEM_LIMIT),
        name="outproj_ffn",
    )(x2, o_diff, o_ret, wo, g2, wup, cw, cb, wd, gf)


def kernel(x, norm_mix_g, w_in, lambda_q1, lambda_k1, lambda_q2, lambda_k2, diff_subln_g,
           ret_gn_g, w_out, norm_ffn_g, w_up, conv_w, conv_b, w_down, final_norm_g):
    batch, seq, _ = x.shape
    x2 = x.reshape(batch * seq, D_MODEL)

    proj = _inproj(x2, norm_mix_g, w_in[0].astype(BF16))
    o_diff = _diff_attention(proj, batch, seq, lambda_q1, lambda_k1, lambda_q2, lambda_k2,
                             diff_subln_g)
    o_ret = _retention(proj, batch, seq, ret_gn_g)

    y = _ffn(x2, o_diff, o_ret, w_out[0].astype(BF16), norm_ffn_g, w_up[0].astype(BF16),
             conv_w[0], conv_b[0].reshape(1, D_FF), w_down[0].astype(BF16),
             final_norm_g.reshape(1, D_MODEL), seq)
    return y.reshape(batch, seq, D_MODEL)
```

```python
import functools
import math

import jax
import jax.numpy as jnp
import numpy as np
from jax import lax
from jax.experimental import pallas as pl
from jax.experimental.pallas import tpu as pltpu

D_MODEL = 1024
CHUNK = 64
DIFF_HEADS = 4
DIFF_HEAD_DIM = 64
DIFF_V_DIM = 128
DIFF_WIDTH = 512
RET_HEADS = 4
RET_KEY_DIM = 64
RET_V_DIM = 128
RET_WIDTH = 512
D_FF = 2816
IN_WIDTH = 3072
EPS = 1e-6
LAM_INIT = 0.8 - 0.6 * math.exp(-0.3 * 0)
LOG2E = math.log2(math.e)
MASKED = -1e30
LANES = 128
GELU_C = math.sqrt(2.0 / math.pi)
GELU_A = 0.044715

COL_DQ, COL_DK, COL_DV = 0, 4, 8
COLB_RQ, COLB_RK = 6, 7
COLB_RV, COLB_RG = 4, 5

TM_IN = 1024
TQ_ATT = 512
TK_ATT = 512
V_ROWS = DIFF_V_DIM + 16
EXP2_UNDERFLOW = 150.0
BOUND_MARGIN = 2.0
NORM_SLACK = 1.001
KNORM_SLACK = 1.002
OWN_SLACK = 2.0 ** -7
T_RET = 256
RET_BLOCKS = 8
TM_FFN = 1024
TF = 256
NF = D_FF // TF
VMEM_LIMIT = 56 * 1024 * 1024

BF16 = jnp.bfloat16
F32 = jnp.float32


def _rms(x, g):
    ms = jnp.mean(x * x, axis=-1, keepdims=True)
    return x * lax.rsqrt(ms + EPS) * g


def _inproj_kernel(x_ref, g_ref, w_ref, o_ref):
    h = _rms(x_ref[...], g_ref[...]).astype(BF16)
    o_ref[...] = jnp.dot(h, w_ref[...], preferred_element_type=F32).astype(o_ref.dtype)


def _inproj(x2, g, w_bf16):
    n = x2.shape[0]
    return pl.pallas_call(
        _inproj_kernel,
        grid=(n // TM_IN,),
        in_specs=[
            pl.BlockSpec((TM_IN, D_MODEL), lambda i: (i, 0)),
            pl.BlockSpec((1, D_MODEL), lambda i: (0, 0)),
            pl.BlockSpec((D_MODEL, IN_WIDTH), lambda i: (0, 0), pipeline_mode=pl.Buffered(1)),
        ],
        out_specs=pl.BlockSpec((TM_IN, IN_WIDTH), lambda i: (i, 0)),
        out_shape=jax.ShapeDtypeStruct((n, IN_WIDTH), BF16),
        compiler_params=pltpu.CompilerParams(
            dimension_semantics=("arbitrary",), vmem_limit_bytes=VMEM_LIMIT),
        name="inproj",
    )(x2, g, w_bf16)


def _attn_kernel(q_ref, k_ref, v_ref, postab_ref, ctab_ref, cpos_ref, corr_ref,
                 lq1_ref, lk1_ref, lq2_ref, lk2_ref, g_ref, o_ref,
                 kaug_scr, vt_scr, qt_scr, first_scr, nxt_scr, s_scr, m_scr, acc_scr):
    TQ, TK = TQ_ATT, TK_ATT
    DV, DH = DIFF_V_DIM, DIFF_HEAD_DIM
    seq = k_ref.shape[0]
    n_tiles = seq // TK
    tile_lane = lax.broadcasted_iota(jnp.int32, (1, LANES), 1)
    lane = lax.broadcasted_iota(jnp.int32, (TK, LANES), 1)
    ctab = ctab_ref[0]
    slope_end = cpos_ref[0, 0:1, :]
    slope = cpos_ref[0, 1:2, :]

    ones_row = lax.broadcasted_iota(jnp.int32, (LANES, LANES), 0)
    ones_col = lax.broadcasted_iota(jnp.int32, (LANES, LANES), 1)
    half_ones = jnp.where((ones_row < DH) == (ones_col < DH), 1.0, 0.0).astype(BF16)
    prefix_max = [jnp.zeros((1, LANES), F32)] * 2
    running = [jnp.zeros((1, 1), F32)] * 2
    firsts = jnp.zeros((1, LANES), jnp.int32)
    for c in range(n_tiles):
        rows = slice(c * TK, (c + 1) * TK)
        kc = k_ref[rows, :]
        pt = postab_ref[rows, :]
        kaug_scr[0, rows, :] = jnp.where(lane < DH, kc, pt)
        kaug_scr[1, rows, :] = jnp.where(lane >= DH, kc, pt)
        vt_scr[c, 0:DV, :] = v_ref[rows, :].T
        vt_scr[c, DV:, :] = jnp.ones((V_ROWS - DV, TK), BF16)
        kf = kc.astype(F32)
        n2 = jnp.max(jnp.dot((kf * kf).astype(BF16), half_ones, preferred_element_type=F32),
                     axis=0, keepdims=True)
        q = q_ref[rows, :].astype(F32) * (DH ** -0.5 * LOG2E)
        own = jnp.min(jnp.dot((q * kf).astype(BF16), half_ones, preferred_element_type=F32),
                      axis=0, keepdims=True)
        skip = tile_lane < c
        for half in range(2):
            in_half = (lane >= DH) if half else (lane < DH)
            in_half_row = (tile_lane >= DH) if half else (tile_lane < DH)
            k_norm = jnp.sqrt(jnp.max(jnp.where(in_half_row, n2, 0.0), axis=1,
                                      keepdims=True)) * KNORM_SLACK
            qb = jnp.where(in_half, q, ctab).astype(BF16).T
            qt_scr[c, half] = qb
            qf = qb[half * DH:(half + 1) * DH, :].astype(F32)
            q_norm = jnp.sqrt(jnp.max(jnp.sum(qf * qf, axis=0, keepdims=True),
                                      axis=1, keepdims=True)) * NORM_SLACK
            own_min = jnp.min(jnp.where(in_half_row, own, jnp.inf), axis=1, keepdims=True)
            upper = q_norm * prefix_max[half] + slope_end
            lower = slope * float(c * TQ) + own_min - OWN_SLACK * (q_norm * k_norm)
            skip = jnp.logical_and(skip, upper + BOUND_MARGIN < lower - EXP2_UNDERFLOW)
            running[half] = jnp.maximum(running[half], k_norm)
            prefix_max[half] = jnp.where(tile_lane == c, running[half], prefix_max[half])
        first_c = jnp.sum(skip.astype(jnp.int32), axis=1, keepdims=True)
        firsts = jnp.where(tile_lane == c, first_c, firsts)
    first_scr[...] = firsts
    acc_scr[...] = jnp.ones(acc_scr.shape, F32)

    lam = (jnp.exp(jnp.sum(lq1_ref[...] * lk1_ref[...], axis=1, keepdims=True))
           - jnp.exp(jnp.sum(lq2_ref[...] * lk2_ref[...], axis=1, keepdims=True))
           + LAM_INIT)

    def scores(i, j, slot, halves=(0, 1)):
        start = pl.multiple_of(j * TK, TK)
        for half in halves:
            s_scr[slot, half] = jnp.dot(kaug_scr[half, pl.ds(start, TK), :], qt_scr[i, half],
                                        preferred_element_type=F32)

    def consume(j, slot, corr, halves=(0, 1)):
        vt = vt_scr[j]
        for half in halves:
            s = s_scr[slot, half]
            if corr is not None:
                s = s + corr
            m_prev = m_scr[half]
            m_new = jnp.maximum(m_prev, jnp.max(s, axis=0, keepdims=True))
            alpha = jnp.exp2(m_prev - m_new)
            p = jnp.exp2(s - m_new).astype(BF16)
            acc_scr[half] = alpha * acc_scr[half] + jnp.dot(
                vt, p, preferred_element_type=F32)
            m_scr[half] = m_new

    def finish(i):
        inv_l0 = 1.0 / acc_scr[0, DV:DV + 1, :]
        inv_l1 = 1.0 / acc_scr[1, DV:DV + 1, :]
        o_t = acc_scr[0, 0:DV, :] * inv_l0 - lam * (acc_scr[1, 0:DV, :] * inv_l1)
        ms = jnp.mean(o_t * o_t, axis=0, keepdims=True)
        y = (o_t * lax.rsqrt(ms + EPS)).T
        row0 = pl.multiple_of(i * TQ, TQ)
        o_ref[pl.ds(row0, TQ), :] = (y * g_ref[...] * (1.0 - LAM_INIT)).astype(o_ref.dtype)

    n_q = seq // TQ

    def q_tile(i, first):
        nxt = jnp.minimum(i + 1, n_q - 1)
        n_past = i - first
        n_rem = n_past - 1

        def start():
            nxt_scr[0] = jnp.sum(jnp.where(tile_lane == nxt, first_scr[...], 0))
            finish(jnp.maximum(i - 1, 0))
            m_scr[...] = jnp.full(m_scr.shape, MASKED, F32)
            acc_scr[...] = jnp.zeros(acc_scr.shape, F32)

        @pl.when(n_past == 0)
        def _():
            start()
            for half in range(2):
                consume(i, 2, corr_ref[0], (half,))
                scores(nxt, nxt_scr[0], 2, (half,))

        @pl.when(n_past == 1)
        def _():
            scores(i, i, 0)
            start()
            for half in range(2):
                consume(first, 2, None, (half,))
                scores(nxt, nxt_scr[0], 2, (half,))
            consume(i, 0, corr_ref[0])

        @pl.when(n_past > 1)
        def _():
            scores(i, first + 1, 0)
            start()
            consume(first, 2, None)

        def tile_group(j, n):
            scores(i, j + 1, 1)
            for u in range(n - 1):
                for half in range(2):
                    consume(j + u, u % 2, None, (half,))
                    scores(i, j + u + 2, u % 2, (half,))
            consume(j + n - 1, (n - 1) % 2, None)

        def group_loop(base, trips, n):
            def body(t, c):
                tile_group(base + n * t, n)
                return c
            lax.fori_loop(0, trips, body, 0)

        first_nxt = nxt_scr[0]
        n_loop = jnp.maximum(n_rem, 0)
        group_loop(first + 1, n_loop // 4, 4)
        group_loop(first + 1 + 4 * (n_loop // 4), (n_loop % 4) // 2, 2)

        @pl.when(jnp.logical_and(n_past > 1, n_rem % 2 == 0))
        def _():
            scores(nxt, first_nxt, 2)
            consume(i, 0, corr_ref[0])

        @pl.when(jnp.logical_and(n_past > 1, n_rem % 2 == 1))
        def _():
            scores(i, i, 1)
            consume(i - 1, 0, None)
            scores(nxt, first_nxt, 2)
            consume(i, 1, corr_ref[0])

        return first_nxt

    scores(0, 0, 2)
    lax.fori_loop(0, n_q, q_tile, jnp.int32(0))
    finish(n_q - 1)


def _bf16_pieces(x, n):
    x = np.asarray(x, np.float32)
    pieces = []
    for _ in range(n):
        p = x.astype(BF16).astype(np.float32)
        pieces.append(p)
        x = x - p
    return pieces


def _attn_tables(seq):
    TQ, TK = TQ_ATT, TK_ATT
    slopes = (np.exp2(-8.0 * np.arange(1, DIFF_HEADS + 1, dtype=np.float64) / DIFF_HEADS)
              * LOG2E).astype(np.float32)
    c1, c2, c3 = _bf16_pieces(slopes, 3)
    ctab = np.zeros((DIFF_HEADS, 1, LANES), np.float32)
    pos = np.arange(seq)
    p_hi, p_lo = (pos // 64) * 64, pos % 64
    postab = np.zeros((seq, LANES), np.float32)
    for base in (0, DIFF_HEAD_DIM):
        for n, (cp, pp) in enumerate(((c1, p_hi), (c2, p_hi), (c3, p_hi),
                                      (c1, p_lo), (c2, p_lo), (c3, p_lo))):
            ctab[:, 0, base + n] = cp
            postab[:, base + n] = pp
    c = np.arange(TK)[:, None]
    r = np.arange(TQ)[None, :]
    allowed = (c // CHUNK) <= (r // CHUNK)
    ahead = np.where(c > r, 2.0 * (r - c), 0.0)
    corr = np.where(allowed[None], slopes.astype(np.float64)[:, None, None] * ahead[None], MASKED)
    cpos = np.zeros((DIFF_HEADS, 2, LANES), np.float64)
    cpos[:, 0, :] = slopes.astype(np.float64)[:, None] * ((np.arange(LANES) + 1) * TK - 1)[None, :]
    cpos[:, 1, :] = slopes.astype(np.float64)[:, None]
    return (jnp.asarray(postab, BF16), jnp.asarray(ctab, F32), jnp.asarray(cpos, F32),
            jnp.asarray(corr, F32))


def _diff_attention(proj, batch, seq, lq1, lk1, lq2, lk2, subln_g):
    TQ, TK = TQ_ATT, TK_ATT
    assert TQ == TK and seq % TK == 0
    assert seq // TK <= LANES
    postab, ctab, cpos, corr = _attn_tables(seq)
    small = lambda shape: pl.BlockSpec(shape, lambda b, h: (0,) * len(shape))
    return pl.pallas_call(
        _attn_kernel,
        grid=(batch, DIFF_HEADS),
        in_specs=[
            pl.BlockSpec((seq, LANES), lambda b, h: (b, COL_DQ + h)),
            pl.BlockSpec((seq, LANES), lambda b, h: (b, COL_DK + h)),
            pl.BlockSpec((seq, LANES), lambda b, h: (b, COL_DV + h)),
            pl.BlockSpec((seq, LANES), lambda b, h: (0, 0), pipeline_mode=pl.Buffered(1)),
            pl.BlockSpec((1, 1, LANES), lambda b, h: (h, 0, 0)),
            pl.BlockSpec((1, 2, LANES), lambda b, h: (h, 0, 0)),
            pl.BlockSpec((1, TK, TQ), lambda b, h: (h, 0, 0)),
            small((1, DIFF_HEAD_DIM)), small((1, DIFF_HEAD_DIM)),
            small((1, DIFF_HEAD_DIM)), small((1, DIFF_HEAD_DIM)),
            small((1, DIFF_V_DIM)),
        ],
        out_specs=pl.BlockSpec((seq, DIFF_V_DIM), lambda b, h: (b, h)),
        out_shape=jax.ShapeDtypeStruct((batch * seq, DIFF_WIDTH), BF16),
        scratch_shapes=[
            pltpu.VMEM((2, seq, LANES), BF16),
            pltpu.VMEM((seq // TK, V_ROWS, TK), BF16),
            pltpu.VMEM((seq // TQ, 2, LANES, TQ), BF16),
            pltpu.VMEM((1, LANES), jnp.int32),
            pltpu.SMEM((1,), jnp.int32),
            pltpu.VMEM((3, 2, TK, TQ), F32),
            pltpu.VMEM((2, 1, TQ), F32),
            pltpu.VMEM((2, V_ROWS, TQ), F32),
        ],
        compiler_params=pltpu.CompilerParams(
            dimension_semantics=("arbitrary", "arbitrary"), vmem_limit_bytes=VMEM_LIMIT),
        name="diff_attn",
    )(proj, proj, proj, postab, ctab, cpos, corr, lq1, lk1, lq2, lk2, subln_g)


def _ret_kernel(q_ref, k_ref, v_ref, gate_ref, dmat_ref, qdec_ref, kdec_ref, cdec_ref,
                gn_ref, o_ref, state_scr):
    T = T_RET

    @pl.when(pl.program_id(1) == 0)
    def _():
        state_scr[...] = jnp.zeros(state_scr.shape, F32)

    lane = lax.broadcasted_iota(jnp.int32, (T, LANES), 1)
    heads = range(RET_HEADS)
    for r in range(RET_BLOCKS):
        rows = slice(r * T, (r + 1) * T)
        qp, km, v, s, outs = {}, {}, {}, {}, {}
        for h in heads:
            pair, half = divmod(h, 2)
            qp[h] = q_ref[rows, LANES * pair:LANES * (pair + 1)]
            kp = k_ref[rows, LANES * pair:LANES * (pair + 1)]
            in_head = (lane >= RET_KEY_DIM) if half else (lane < RET_KEY_DIM)
            km[h] = jnp.where(in_head, kp, jnp.zeros_like(kp))
            v[h] = v_ref[rows, RET_V_DIM * h:RET_V_DIM * (h + 1)]
            s[h] = lax.dot_general(qp[h], km[h], (((1,), (1,)), ((), ())),
                                   preferred_element_type=F32)
        for h in heads:
            state = state_scr[h]
            cross = jnp.dot(qp[h], state.astype(BF16), preferred_element_type=F32)
            intra = jnp.dot((s[h] * dmat_ref[h]).astype(BF16), v[h],
                            preferred_element_type=F32)
            outs[h] = intra + qdec_ref[h] * cross
            vd = (v[h].astype(F32) * kdec_ref[h]).astype(BF16)
            state_scr[h] = cdec_ref[h] * state + lax.dot_general(
                km[h], vd, (((0,), (0,)), ((), ())), preferred_element_type=F32)
        for h in heads:
            cols = slice(RET_V_DIM * h, RET_V_DIM * (h + 1))
            o = outs[h]
            mu = jnp.mean(o, axis=-1, keepdims=True)
            d = o - mu
            var = jnp.mean(d * d, axis=-1, keepdims=True)
            y = d * lax.rsqrt(var + EPS) * gn_ref[:, cols]
            gate = gate_ref[rows, cols].astype(F32)
            y = y * (gate * (1.0 / (1.0 + jnp.exp(-gate))))
            o_ref[rows, cols] = y.astype(o_ref.dtype)


def _ret_tables():
    T = T_RET
    log_g = np.log1p(-np.exp2(-5.0 - np.arange(RET_HEADS, dtype=np.float64)))
    n = np.arange(T, dtype=np.float64)
    kscale = RET_KEY_DIM ** -0.5
    allowed = (np.arange(T)[None, :] // CHUNK) <= (np.arange(T)[:, None] // CHUNK)
    dmat = np.where(allowed[None],
                    np.exp(log_g[:, None, None] * np.abs(n[:, None] - n[None, :])[None]), 0.0) * kscale
    qdec = np.exp(log_g[:, None] * (n + 1.0)[None]) * kscale
    kdec = np.exp(log_g[:, None] * (T - 1.0 - n)[None])
    cdec = np.exp(log_g * T)
    qdec = np.broadcast_to(qdec[:, :, None], (RET_HEADS, T, RET_V_DIM))
    kdec = np.broadcast_to(kdec[:, :, None], (RET_HEADS, T, RET_V_DIM))
    cdec = np.broadcast_to(cdec[:, None, None], (RET_HEADS, 1, RET_V_DIM))
    return tuple(jnp.asarray(a, F32) for a in (dmat, qdec, kdec, cdec))


def _retention(proj, batch, seq, gn_g):
    T = T_RET
    TS = T * RET_BLOCKS
    nt = seq // TS
    dmat, qdec, kdec, cdec = _ret_tables()
    const = lambda shape: pl.BlockSpec(shape, lambda b, t: (0,) * len(shape))
    return pl.pallas_call(
        _ret_kernel,
        grid=(batch, nt),
        in_specs=[
            pl.BlockSpec((TS, 256), lambda b, t: (b * nt + t, COLB_RQ)),
            pl.BlockSpec((TS, 256), lambda b, t: (b * nt + t, COLB_RK)),
            pl.BlockSpec((TS, 512), lambda b, t: (b * nt + t, COLB_RV)),
            pl.BlockSpec((TS, 512), lambda b, t: (b * nt + t, COLB_RG)),
            const((RET_HEADS, T, T)), const((RET_HEADS, T, RET_V_DIM)),
            const((RET_HEADS, T, RET_V_DIM)), const((RET_HEADS, 1, RET_V_DIM)),
            const((1, RET_WIDTH)),
        ],
        out_specs=pl.BlockSpec((TS, RET_WIDTH), lambda b, t: (b * nt + t, 0)),
        out_shape=jax.ShapeDtypeStruct((batch * seq, RET_WIDTH), BF16),
        scratch_shapes=[pltpu.VMEM((RET_HEADS, LANES, RET_V_DIM), F32)],
        compiler_params=pltpu.CompilerParams(
            dimension_semantics=("arbitrary", "arbitrary"), vmem_limit_bytes=VMEM_LIMIT),
        name="retention",
    )(proj, proj, proj, proj, dmat, qdec, kdec, cdec, gn_g)


def _ffn_kernel(tiles_per_seq, x_ref, od_ref, or_ref, wo_ref, g2_ref, wup_ref,
                cw_ref, cb_ref, wd_ref, gf_ref, o_ref, xn_scr, h_scr, halo_scr):
    TM = TM_FFN

    @pl.when(pl.program_id(0) % tiles_per_seq == 0)
    def _():
        halo_scr[...] = jnp.zeros(halo_scr.shape, F32)

    SUB = TM // 2
    row8 = lax.broadcasted_iota(jnp.int32, (8, TF), 0)
    halos = [halo_scr[c] for c in range(NF)]
    groups = (slice(0, SUB), slice(SUB, TM))

    def out_proj(rows):
        x1 = (x_ref[rows, :]
              + jnp.dot(od_ref[rows, :], wo_ref[0:DIFF_WIDTH, :], preferred_element_type=F32)
              + jnp.dot(or_ref[rows, :], wo_ref[DIFF_WIDTH:, :], preferred_element_type=F32))
        o_ref[rows, :] = x1

    def pre_norm(rows):
        xn_scr[rows, :] = _rms(o_ref[rows, :], g2_ref[...]).astype(BF16)

    def chunk(rows, c):
        xn = xn_scr[rows, :]
        cols = slice(c * TF, (c + 1) * TF)
        gate_cols = slice(D_FF + c * TF, D_FF + (c + 1) * TF)
        a = jnp.dot(xn, wup_ref[:, cols], preferred_element_type=F32)
        b = jnp.dot(xn, wup_ref[:, gate_cols], preferred_element_type=F32)
        halo = halos[c]
        halos[c] = a[SUB - 8:SUB, :]
        w0, w1, w2, bias = cw_ref[0:1, cols], cw_ref[1:2, cols], cw_ref[2:3, cols], cb_ref[:, cols]
        u = w0 * pltpu.roll(a, 2, 0) + w1 * pltpu.roll(a, 1, 0) + w2 * a + bias
        top = a[0:8, :]
        top1 = jnp.where(row8 == 0, halo[7:8, :], pltpu.roll(top, 1, 0))
        top2 = jnp.where(row8 == 0, halo[6:7, :],
                         jnp.where(row8 == 1, halo[7:8, :], pltpu.roll(top, 2, 0)))
        u = jnp.concatenate([w0 * top2 + w1 * top1 + w2 * top + bias, u[8:, :]], axis=0)
        gelu2 = u * (1.0 + jnp.tanh(u * (GELU_C + (GELU_C * GELU_A) * (u * u))))
        h_scr[rows, cols] = (gelu2 * b).astype(BF16)

    def down_proj(rows):
        o_ref[rows, :] = o_ref[rows, :] + jnp.dot(h_scr[rows, :], wd_ref[...],
                                                  preferred_element_type=F32)

    def post_norm(rows):
        o_ref[rows, :] = _rms(o_ref[rows, :], gf_ref[...])

    g0, g1 = groups
    out_proj(g0)
    out_proj(g1)
    pre_norm(g0)
    chunk(g0, 0)
    pre_norm(g1)
    for c in range(1, NF):
        chunk(g0, c)
    down_proj(g0)
    chunk(g1, 0)
    post_norm(g0)
    for c in range(1, NF):
        chunk(g1, c)
    down_proj(g1)
    post_norm(g1)
    for c in range(NF):
        halo_scr[c] = halos[c]


def _ffn(x2, o_diff, o_ret, wo, g2, wup, cw, cb, wd, gf, seq):
    n = x2.shape[0]
    TM = TM_FFN
    const = lambda shape: pl.BlockSpec(shape, lambda i: (0,) * len(shape),
                                       pipeline_mode=pl.Buffered(1))
    return pl.pallas_call(
        functools.partial(_ffn_kernel, seq // TM),
        grid=(n // TM,),
        in_specs=[
            pl.BlockSpec((TM, D_MODEL), lambda i: (i, 0)),
            pl.BlockSpec((TM, DIFF_WIDTH), lambda i: (i, 0)),
            pl.BlockSpec((TM, RET_WIDTH), lambda i: (i, 0)),
            const((D_MODEL, D_MODEL)), const((1, D_MODEL)),
            const((D_MODEL, 2 * D_FF)),
            const((3, D_FF)), const((1, D_FF)),
            const((D_FF, D_MODEL)), const((1, D_MODEL)),
        ],
        out_specs=pl.BlockSpec((TM, D_MODEL), lambda i: (i, 0)),
        out_shape=jax.ShapeDtypeStruct((n, D_MODEL), F32),
        scratch_shapes=[
            pltpu.VMEM((TM, D_MODEL), BF16),
            pltpu.VMEM((TM, D_FF), BF16),
            pltpu.VMEM((NF, 8, TF), F32),
        ],
        compiler_params=pltpu.CompilerParams(
            dimension_semantics=("arbitrary",), vmem_limit_bytes=VMEM_LIMIT),
        name="outproj_ffn",
    )(x2, o_diff, o_ret, wo, g2, wup, cw, cb, wd, gf)


def kernel(x, norm_mix_g, w_in, lambda_q1, lambda_k1, lambda_q2, lambda_k2, diff_subln_g,
           ret_gn_g, w_out, norm_ffn_g, w_up, conv_w, conv_b, w_down, final_norm_g):
    batch, seq, _ = x.shape
    x2 = x.reshape(batch * seq, D_MODEL)

    proj = _inproj(x2, norm_mix_g, w_in[0].astype(BF16))
    o_diff = _diff_attention(proj, batch, seq, lambda_q1, lambda_k1, lambda_q2, lambda_k2,
                             diff_subln_g)
    o_ret = _retention(proj, batch, seq, ret_gn_g)

    y = _ffn(x2, o_diff, o_ret, w_out[0].astype(BF16), norm_ffn_g, w_up[0].astype(BF16),
             conv_w[0], conv_b[0].reshape(1, D_FF), (0.5 * w_down[0]).astype(BF16),
             final_norm_g.reshape(1, D_MODEL), seq)
    return y.reshape(batch, seq, D_MODEL)
```
